```python
import jax, jax.numpy as jnp
from jax import lax
import numpy as np

D_MODEL = 1024
BATCH = 8
SEQ = 2048
DEPTH = 1

MEM_LEN = 256
HEAD_DIM = 64
Q_BLOCK = 128
EPS = 1e-6
SB_HEADS = 8
SB_WIDTH = SB_HEADS * HEAD_DIM
DIL_CONFIG = ((128, 1), (512, 4), (2048, 16))
N_DIL_GROUPS = 3
DIL_HEADS_PER_GROUP = 4
DIL_HEADS = N_DIL_GROUPS * DIL_HEADS_PER_GROUP
DIL_WIDTH = DIL_HEADS * HEAD_DIM
DIL_OUT_WIDTH = DIL_HEADS_PER_GROUP * HEAD_DIM
DIL_STREAM_KEYS = 128
ALIBI_MAX_BIAS = 8.0
MEM_HEADS = 4
MEM_HEAD_DIM = 128
MEM_WIDTH = MEM_HEADS * MEM_HEAD_DIM
N_BRANCHES = 3
IN_WIDTH = 3 * SB_WIDTH + 3 * DIL_WIDTH + MEM_WIDTH
IN_SPLITS = (SB_WIDTH, 2 * SB_WIDTH, 3 * SB_WIDTH,
             3 * SB_WIDTH + DIL_WIDTH, 3 * SB_WIDTH + 2 * DIL_WIDTH, 3 * SB_WIDTH + 3 * DIL_WIDTH)
PEER_HEADS = 8
PEER_KEYS = 128
PEER_EXPERTS = PEER_KEYS * PEER_KEYS
PEER_TOPK = 16
PEER_QDIM = 256
PEER_HALF = PEER_QDIM // 2
PEER_CHUNK = 128

kernel_name = 'hybrid_sb_dilated_mem_peer_block'


def rms_norm(x, g):
    xf = x.astype(jnp.float32)
    y = xf * lax.rsqrt(jnp.mean(xf * xf, axis=-1, keepdims=True) + EPS)
    return (y * g.astype(jnp.float32)).astype(x.dtype)


def alibi_slopes(n):
    return jnp.exp2(-ALIBI_MAX_BIAS * jnp.arange(1, n + 1, dtype=jnp.float32) / n)


def stick_breaking_attention(q, k, v):
    B, H, S, dh = q.shape
    nb = S // Q_BLOCK
    q_blocks = q.reshape(B, H, nb, Q_BLOCK, dh).transpose(2, 0, 1, 3, 4)
    kpos = jnp.arange(S)
    scale = dh ** -0.5

    def block(args):
        q_blk, n = args
        z = jnp.einsum('bhqd,bhkd->bhqk', q_blk, k).astype(jnp.float32) * scale
        qpos = n * Q_BLOCK + jnp.arange(Q_BLOCK)
        past = kpos[None, :] < qpos[:, None]
        log_keep = jnp.where(past, jax.nn.log_sigmoid(-z), 0.0)
        between = lax.cumsum(log_keep, axis=3, reverse=True) - log_keep
        a = jnp.where(past, jnp.exp(jax.nn.log_sigmoid(z) + between), 0.0)
        return jnp.einsum('bhqk,bhkd->bhqd', a.astype(v.dtype), v)

    o = lax.map(block, (q_blocks, jnp.arange(nb)))
    return o.transpose(1, 2, 0, 3, 4).reshape(B, H, S, dh)


def dilated_group_attention(q, k, v, dilation, slopes):
    B, S, H, dh = q.shape
    L = S // dilation
    nb = -(-L // Q_BLOCK)
    Lp = nb * Q_BLOCK

    def to_streams(t):
        t = t.reshape(B, L, dilation, H, dh).transpose(0, 2, 3, 1, 4)
        t = jnp.pad(t, ((0, 0), (0, 0), (0, 0), (0, Lp - L), (0, 0)))
        return t.reshape(B, dilation, H, nb, Q_BLOCK, dh)

    qs, ks, vs = to_streams(q), to_streams(k), to_streams(v)

    def with_prev(t):
        prev = jnp.pad(t, ((0, 0), (0, 0), (0, 0), (1, 0), (0, 0), (0, 0)))[:, :, :, :nb]
        return jnp.concatenate([prev, t], axis=4)

    kc, vc = with_prev(ks), with_prev(vs)
    qi = jnp.arange(Q_BLOCK)
    kj = jnp.arange(2 * Q_BLOCK) - Q_BLOCK
    gap = qi[:, None] - kj[None, :]
    blk = jnp.arange(nb)
    valid = ((gap >= 0) & (gap <= DIL_STREAM_KEYS))[None] & \
            ((blk[:, None, None] * Q_BLOCK + kj[None, None, :]) >= 0)
    s = jnp.einsum('brhnqd,brhnkd->brhnqk', qs, kc).astype(jnp.float32) * dh ** -0.5
    s = s - slopes[:, None, None, None] * (gap * dilation).astype(jnp.float32)
    s = jnp.where(valid, s, -1e30)
    lse = jax.nn.logsumexp(s, axis=-1)
    p = jnp.exp(s - lse[..., None])
    o = jnp.einsum('brhnqk,brhnkd->brhnqd', p.astype(v.dtype), vc)
    o = o.reshape(B, dilation, H, Lp, dh)[:, :, :, :L].transpose(0, 3, 1, 2, 4).reshape(B, S, H, dh)
    lse = lse.reshape(B, dilation, H, Lp)[..., :L].transpose(0, 3, 1, 2).reshape(B, S, H)
    return o, lse


def dilated_mixture(q, k, v, g_q, g_k):
    q = rms_norm(q, g_q)
    k = rms_norm(k, g_k)
    slopes = alibi_slopes(DIL_HEADS)
    outs, lses = [], []
    for g, (window, dilation) in enumerate(DIL_CONFIG):
        hs = slice(g * DIL_HEADS_PER_GROUP, (g + 1) * DIL_HEADS_PER_GROUP)
        o, lse = dilated_group_attention(q[:, :, hs], k[:, :, hs], v[:, :, hs], dilation, slopes[hs])
        outs.append(o)
        lses.append(lse)
    w = jax.nn.softmax(jnp.stack(lses, axis=2), axis=2)
    o = jnp.stack(outs, axis=2)
    return jnp.sum(w[..., None].astype(o.dtype) * o, axis=2)


def memory_attention(q, mem_h, w_kv, g_q, g_k):
    B, M, _ = mem_h.shape
    kv = (mem_h @ w_kv).reshape(B, M, 2, MEM_HEADS, MEM_HEAD_DIM)
    k = rms_norm(kv[:, :, 0], g_k)
    v = kv[:, :, 1]
    q = rms_norm(q, g_q)
    s = jnp.einsum('bshd,bmhd->bhsm', q, k).astype(jnp.float32) * MEM_HEAD_DIM ** -0.5
    p = jax.nn.softmax(s, axis=-1)
    return jnp.einsum('bhsm,bmhd->bshd', p.astype(v.dtype), v)


def peer_ffn(h, w_q, subkeys, u, v):
    B, S, D = h.shape
    T = B * S
    hf = h.reshape(T, D)
    q = (hf @ w_q).reshape(T, PEER_HEADS, 2, PEER_HALF)
    sc = jnp.einsum('thpd,hpkd->thpk', q, subkeys).astype(jnp.float32)
    s1, i1 = lax.top_k(sc[:, :, 0], PEER_TOPK)
    s2, i2 = lax.top_k(sc[:, :, 1], PEER_TOPK)
    cand = (s1[..., :, None] + s2[..., None, :]).reshape(T, PEER_HEADS, PEER_TOPK * PEER_TOPK)
    cand_idx = (i1[..., :, None] * PEER_KEYS + i2[..., None, :]).reshape(T, PEER_HEADS, PEER_TOPK * PEER_TOPK)
    top_s, pos = lax.top_k(cand, PEER_TOPK)
    idx = jnp.take_along_axis(cand_idx, pos, axis=-1)
    gate = jax.nn.softmax(top_s, axis=-1)
    nc = T // PEER_CHUNK

    def chunk(args):
        hc, ic, gc = args
        act = jax.nn.gelu(jnp.einsum('chkd,cd->chk', u[ic], hc).astype(jnp.float32), approximate=False)
        w = (gc * act).astype(h.dtype)
        return jnp.einsum('chk,chkd->cd', w, v[ic])

    out = lax.map(chunk, (hf.reshape(nc, PEER_CHUNK, D),
                          idx.reshape(nc, PEER_CHUNK, PEER_HEADS, PEER_TOPK),
                          gate.reshape(nc, PEER_CHUNK, PEER_HEADS, PEER_TOPK)))
    return out.reshape(B, S, D)


def setup_inputs(seed: int = 0) -> dict:
    key = jax.random.key(seed)
    ks = jax.random.split(key, 24)
    D = D_MODEL

    def nrm(k, shape, scale):
        return jax.random.normal(k, shape, jnp.float32) * scale

    def gain(k, n):
        return 1.0 + 0.02 * jax.random.normal(k, (DEPTH, n), jnp.float32)

    return {
        'x': nrm(ks[0], (BATCH, SEQ, D), 1.0),
        'mem': nrm(ks[1], (BATCH, MEM_LEN, D), 1.0),
        'g_mix': gain(ks[2], D),
        'g_mem': gain(ks[3], D),
        'w_in': nrm(ks[4], (DEPTH, D, IN_WIDTH), D ** -0.5),
        'w_mem_kv': nrm(ks[5], (DEPTH, D, 2 * MEM_WIDTH), D ** -0.5),
        'g_q_dil': gain(ks[6], HEAD_DIM),
        'g_k_dil': gain(ks[7], HEAD_DIM),
        'g_q_mem': gain(ks[8], MEM_HEAD_DIM),
        'g_k_mem': gain(ks[9], MEM_HEAD_DIM),
        'w_o_sb': nrm(ks[10], (DEPTH, SB_WIDTH, D), SB_WIDTH ** -0.5),
        'w_o_dil': nrm(ks[11], (DEPTH, DIL_OUT_WIDTH, D), DIL_OUT_WIDTH ** -0.5),
        'w_o_mem': nrm(ks[12], (DEPTH, MEM_WIDTH, D), MEM_WIDTH ** -0.5),
        'w_gate': nrm(ks[13], (DEPTH, D, N_BRANCHES * D), D ** -0.5),
        'b_gate': nrm(ks[14], (DEPTH, N_BRANCHES * D), 0.02),
        'w_out': nrm(ks[15], (DEPTH, D, D), D ** -0.5),
        'g_ffn': gain(ks[16], D),
        'w_peer_q': nrm(ks[17], (DEPTH, D, PEER_HEADS * PEER_QDIM), D ** -0.5),
        'peer_subkeys': nrm(ks[18], (DEPTH, PEER_HEADS, 2, PEER_KEYS, PEER_HALF), PEER_HALF ** -0.5),
        'peer_u': nrm(ks[19], (DEPTH, PEER_EXPERTS, D), D ** -0.5),
        'peer_v': nrm(ks[20], (DEPTH, PEER_EXPERTS, D), (PEER_HEADS * PEER_TOPK) ** -0.5),
    }


def reference(x, mem, g_mix, g_mem, w_in, w_mem_kv, g_q_dil, g_k_dil, g_q_mem, g_k_mem,
              w_o_sb, w_o_dil, w_o_mem, w_gate, b_gate, w_out, g_ffn, w_peer_q,
              peer_subkeys, peer_u, peer_v):
    B, S, D = x.shape
    for l in range(DEPTH):
        h = rms_norm(x, g_mix[l])
        proj = h @ w_in[l]
        sb_q, sb_k, sb_v, d_q, d_k, d_v, m_q = jnp.split(proj, IN_SPLITS, axis=-1)

        def sb_heads(t):
            return t.reshape(B, S, SB_HEADS, HEAD_DIM).transpose(0, 2, 1, 3)

        y_sb = stick_breaking_attention(sb_heads(sb_q), sb_heads(sb_k), sb_heads(sb_v))
        y_sb = y_sb.transpose(0, 2, 1, 3).reshape(B, S, SB_WIDTH) @ w_o_sb[l]

        def dil_heads(t):
            return t.reshape(B, S, DIL_HEADS, HEAD_DIM)

        y_dil = dilated_mixture(dil_heads(d_q), dil_heads(d_k), dil_heads(d_v), g_q_dil[l], g_k_dil[l])
        y_dil = y_dil.reshape(B, S, DIL_OUT_WIDTH) @ w_o_dil[l]

        mem_h = rms_norm(mem, g_mem[l])
        y_mem = memory_attention(m_q.reshape(B, S, MEM_HEADS, MEM_HEAD_DIM), mem_h, w_mem_kv[l],
                                 g_q_mem[l], g_k_mem[l])
        y_mem = y_mem.reshape(B, S, MEM_WIDTH) @ w_o_mem[l]

        gates = jax.nn.sigmoid(h @ w_gate[l] + b_gate[l]).reshape(B, S, N_BRANCHES, D)
        merged = gates[:, :, 0] * y_sb + gates[:, :, 1] * y_dil + gates[:, :, 2] * y_mem
        x = x + merged @ w_out[l]

        h2 = rms_norm(x, g_ffn[l])
        x = x + peer_ffn(h2, w_peer_q[l], peer_subkeys[l], peer_u[l], peer_v[l])
    return x
```

```python
import functools
import math

import numpy as np
import jax
import jax.numpy as jnp
from jax import lax
from jax.experimental import pallas as pl
from jax.experimental.pallas import tpu as pltpu

F32 = jnp.float32
BF16 = jnp.bfloat16

EPS = 1e-6
HEAD_DIM = 64
LANES = 128
SB_HEADS = 8
SB_WIDTH = SB_HEADS * HEAD_DIM
DIL_CONFIG = ((128, 1), (512, 4), (2048, 16))
DIL_HEADS_PER_GROUP = 4
DIL_HEADS = len(DIL_CONFIG) * DIL_HEADS_PER_GROUP
DIL_WIDTH = DIL_HEADS * HEAD_DIM
DIL_OUT_WIDTH = DIL_HEADS_PER_GROUP * HEAD_DIM
DIL_BLOCK = 128
ALIBI_MAX_BIAS = 8.0
MEM_HEADS = 4
MEM_HEAD_DIM = 128
MEM_WIDTH = MEM_HEADS * MEM_HEAD_DIM
PEER_HEADS = 8
PEER_KEYS = 128
PEER_TOPK = 16
PEER_HALF = 128
NEG_BIG = -1e30
VMEM_LIMIT = 56 * 1024 * 1024


def _cparams(sem):
    return pltpu.CompilerParams(dimension_semantics=sem, vmem_limit_bytes=VMEM_LIMIT)


def _rms(x, g):
    ms = jnp.mean(x * x, axis=-1, keepdims=True)
    return x * lax.rsqrt(ms + EPS) * g


def _norm_matmul_kernel(x_ref, g_ref, w_ref, o_ref, h_ref):
    @pl.when(pl.program_id(1) == 0)
    def _():
        h_ref[...] = _rms(x_ref[...], g_ref[...]).astype(BF16)

    o_ref[...] = jnp.dot(h_ref[...], w_ref[...], preferred_element_type=F32).astype(o_ref.dtype)


def _norm_matmul(x2d, g, w, out_dtype, tm, tn):
    m, k = x2d.shape
    n = w.shape[1]
    return pl.pallas_call(
        _norm_matmul_kernel,
        grid=(m // tm, n // tn),
        in_specs=[pl.BlockSpec((tm, k), lambda i, j: (i, 0)),
                  pl.BlockSpec((1, k), lambda i, j: (0, 0)),
                  pl.BlockSpec((k, tn), lambda i, j: (0, j))],
        out_specs=pl.BlockSpec((tm, tn), lambda i, j: (i, j)),
        out_shape=jax.ShapeDtypeStruct((m, n), out_dtype),
        scratch_shapes=[pltpu.VMEM((tm, k), BF16)],
        compiler_params=_cparams(("parallel", "arbitrary")),
        name="norm_matmul",
    )(x2d, g.reshape(1, k), w)


def _sb_kernel(q_ref, k_ref, v_ref, o_ref, *, tq, scale):
    qi = pl.program_id(2)
    q2 = q_ref[...]
    lane = lax.broadcasted_iota(jnp.int32, (tq, LANES), 1)
    row = lax.broadcasted_iota(jnp.int32, (tq, tq), 0)
    col = lax.broadcasted_iota(jnp.int32, (tq, tq), 1)
    later = (row > col).astype(BF16)
    outs = []
    for half in range(2):
        in_head = (lane < HEAD_DIM) if half == 0 else (lane >= HEAD_DIM)
        qm = jnp.where(in_head, q2, jnp.zeros_like(q2))

        def body(jj, carry, qm=qm):
            acc, newer = carry
            j = qi - jj
            start = pl.multiple_of(j * tq, tq)
            kb = k_ref[pl.ds(start, tq), :]
            vb = v_ref[pl.ds(start, tq), :]
            z = lax.dot_general(qm, kb, (((1,), (1,)), ((), ())),
                                preferred_element_type=F32) * scale
            sp = jnp.maximum(z, 0.0) + jnp.log1p(jnp.exp(-jnp.abs(z)))
            past = (j * tq + col) < (qi * tq + row)
            log_keep = jnp.where(past, -sp, 0.0)
            local = jnp.dot(log_keep.astype(BF16), later, preferred_element_type=F32)
            a = jnp.where(past, jnp.exp((z - sp) + (local + newer)), 0.0)
            acc = acc + jnp.dot(a.astype(BF16), vb, preferred_element_type=F32)
            newer = newer + jnp.sum(log_keep, axis=-1, keepdims=True)
            return acc, newer

        acc, _ = lax.fori_loop(0, qi + 1, body,
                               (jnp.zeros((tq, LANES), F32), jnp.zeros((tq, 1), F32)))
        outs.append(acc)
    o_ref[...] = jnp.where(lane < HEAD_DIM, outs[0], outs[1]).astype(o_ref.dtype)


def _sb_attention(sb, batch, seq, tq=128):
    pairs = SB_WIDTH // LANES
    return pl.pallas_call(
        functools.partial(_sb_kernel, tq=tq, scale=HEAD_DIM ** -0.5),
        grid=(batch, pairs, seq // tq),
        in_specs=[pl.BlockSpec((None, tq, LANES), lambda b, p, i: (b, i, p)),
                  pl.BlockSpec((None, seq, LANES), lambda b, p, i: (b, 0, pairs + p)),
                  pl.BlockSpec((None, seq, LANES), lambda b, p, i: (b, 0, 2 * pairs + p))],
        out_specs=pl.BlockSpec((None, tq, LANES), lambda b, p, i: (b, i, p)),
        out_shape=jax.ShapeDtypeStruct((batch, seq, SB_WIDTH), BF16),
        compiler_params=_cparams(("parallel", "parallel", "parallel")),
        name="sb_attention",
    )(sb, sb, sb)


def _head_rms(x, g, lane):
    xx = x * x
    lo = lane < HEAD_DIM
    s_lo = jnp.sum(jnp.where(lo, xx, 0.0), axis=-1, keepdims=True)
    s_hi = jnp.sum(jnp.where(lo, 0.0, xx), axis=-1, keepdims=True)
    ms = jnp.where(lo, s_lo, s_hi) * (1.0 / HEAD_DIM)
    return x * lax.rsqrt(ms + EPS) * g


def _dil_kernel(q_ref, k_ref, v_ref, gq_ref, gk_ref, o_ref, l_ref, qn_ref, kn_ref, vn_ref,
                *, dilation, length, slopes, scale):
    nb = length // DIL_BLOCK
    pair = pl.program_id(1)
    lane_l = lax.broadcasted_iota(jnp.int32, (length, LANES), 1)
    lane_b = lax.broadcasted_iota(jnp.int32, (DIL_BLOCK, LANES), 1)
    qi = lax.broadcasted_iota(jnp.int32, (DIL_BLOCK, 2 * DIL_BLOCK), 0)
    kj = lax.broadcasted_iota(jnp.int32, (DIL_BLOCK, 2 * DIL_BLOCK), 1) - DIL_BLOCK
    gap = qi - kj
    band = (gap >= 0) & (gap <= DIL_BLOCK)
    gapf = (gap * dilation).astype(F32)
    kn_ref[pl.ds(0, DIL_BLOCK), :] = jnp.zeros((DIL_BLOCK, LANES), BF16)
    vn_ref[pl.ds(0, DIL_BLOCK), :] = jnp.zeros((DIL_BLOCK, LANES), BF16)

    def stream(c, _):
        if dilation == 1:
            rows = pl.ds(0, length)
        else:
            rows = pl.ds(c, length, stride=dilation)
        qn_ref[...] = _head_rms(q_ref[rows, :], gq_ref[...], lane_l).astype(BF16)
        kn_ref[pl.ds(DIL_BLOCK, length), :] = _head_rms(k_ref[rows, :], gk_ref[...], lane_l).astype(BF16)
        vn_ref[pl.ds(DIL_BLOCK, length), :] = v_ref[rows, :].astype(BF16)

        def block(n, _):
            start = pl.multiple_of(n * DIL_BLOCK, DIL_BLOCK)
            qb = qn_ref[pl.ds(start, DIL_BLOCK), :]
            kw = kn_ref[pl.ds(start, 2 * DIL_BLOCK), :]
            vw = vn_ref[pl.ds(start, 2 * DIL_BLOCK), :]
            valid = band & ((n * DIL_BLOCK + kj) >= 0)
            outs, lses = [], []
            for half in range(2):
                in_head = (lane_b < HEAD_DIM) if half == 0 else (lane_b >= HEAD_DIM)
                qm = jnp.where(in_head, qb, jnp.zeros_like(qb))
                slope = jnp.where(pair == 0, slopes[half], slopes[2 + half])
                s = lax.dot_general(qm, kw, (((1,), (1,)), ((), ())),
                                    preferred_element_type=F32) * scale
                s = jnp.where(valid, s - slope * gapf, NEG_BIG)
                m = jnp.max(s, axis=-1, keepdims=True)
                e = jnp.exp(s - m)
                denom = jnp.sum(e, axis=-1, keepdims=True)
                o = jnp.dot(e.astype(BF16), vw, preferred_element_type=F32) / denom
                outs.append(o)
                lses.append(m + jnp.log(denom))
            lo = lane_b < HEAD_DIM
            o2 = jnp.where(lo, outs[0], outs[1])
            l2 = jnp.where(lo, lses[0], lses[1])
            if dilation == 1:
                dst = pl.ds(start, DIL_BLOCK)
            else:
                dst = pl.ds(c + start * dilation, DIL_BLOCK, stride=dilation)
            o_ref[dst, :] = o2
            l_ref[dst, :] = l2
            return 0

        lax.fori_loop(0, nb, block, 0)
        return 0

    lax.fori_loop(0, dilation, stream, 0)


def _dil_group(dil, g_q2, g_k2, group, batch, seq):
    _, dilation = DIL_CONFIG[group]
    length = seq // dilation
    blocks = DIL_WIDTH // LANES
    pairs = DIL_OUT_WIDTH // LANES
    all_slopes = [2.0 ** (-ALIBI_MAX_BIAS * (i + 1) / DIL_HEADS) for i in range(DIL_HEADS)]
    slopes = tuple(float(np.float32(s)) for s in
                   all_slopes[group * DIL_HEADS_PER_GROUP:(group + 1) * DIL_HEADS_PER_GROUP])
    kern = functools.partial(_dil_kernel, dilation=dilation, length=length, slopes=slopes,
                             scale=HEAD_DIM ** -0.5)
    col = lambda off: (lambda b, p: (b, 0, off + group * pairs + p))
    out_sd = jax.ShapeDtypeStruct((batch, seq, DIL_OUT_WIDTH), F32)
    return pl.pallas_call(
        kern,
        grid=(batch, pairs),
        in_specs=[pl.BlockSpec((None, seq, LANES), col(0)),
                  pl.BlockSpec((None, seq, LANES), col(blocks)),
                  pl.BlockSpec((None, seq, LANES), col(2 * blocks)),
                  pl.BlockSpec((1, LANES), lambda b, p: (0, 0)),
                  pl.BlockSpec((1, LANES), lambda b, p: (0, 0))],
        out_specs=[pl.BlockSpec((None, seq, LANES), lambda b, p: (b, 0, p)),
                   pl.BlockSpec((None, seq, LANES), lambda b, p: (b, 0, p))],
        out_shape=[out_sd, out_sd],
        scratch_shapes=[pltpu.VMEM((length, LANES), BF16),
                        pltpu.VMEM((length + DIL_BLOCK, LANES), BF16),
                        pltpu.VMEM((length + DIL_BLOCK, LANES), BF16)],
        compiler_params=_cparams(("parallel", "parallel")),
        name=f"dilated_attention_r{dilation}",
    )(dil, dil, dil, g_q2, g_k2)


def _mem_kernel(q_ref, k_ref, v_ref, gq_ref, gk_ref, o_ref, *, scale):
    q = _rms(q_ref[...], gq_ref[...]).astype(BF16)
    k = _rms(k_ref[...], gk_ref[...]).astype(BF16)
    s = lax.dot_general(q, k, (((1,), (1,)), ((), ())), preferred_element_type=F32) * scale
    m = jnp.max(s, axis=-1, keepdims=True)
    e = jnp.exp(s - m)
    denom = jnp.sum(e, axis=-1, keepdims=True)
    o = jnp.dot(e.astype(BF16), v_ref[...].astype(BF16), preferred_element_type=F32) / denom
    o_ref[...] = o.astype(o_ref.dtype)


def _mem_attention(mq, kv, g_q, g_k, batch, seq, mem_len, tq=512):
    return pl.pallas_call(
        functools.partial(_mem_kernel, scale=MEM_HEAD_DIM ** -0.5),
        grid=(batch, MEM_HEADS, seq // tq),
        in_specs=[pl.BlockSpec((None, tq, LANES), lambda b, h, i: (b, i, h)),
                  pl.BlockSpec((None, mem_len, LANES), lambda b, h, i: (b, 0, h)),
                  pl.BlockSpec((None, mem_len, LANES), lambda b, h, i: (b, 0, MEM_HEADS + h)),
                  pl.BlockSpec((1, LANES), lambda b, h, i: (0, 0)),
                  pl.BlockSpec((1, LANES), lambda b, h, i: (0, 0))],
        out_specs=pl.BlockSpec((None, tq, LANES), lambda b, h, i: (b, i, h)),
        out_shape=jax.ShapeDtypeStruct((batch, seq, MEM_WIDTH), BF16),
        compiler_params=_cparams(("parallel", "parallel", "parallel")),
        name="memory_attention",
    )(mq, kv, kv, g_q.reshape(1, LANES), g_k.reshape(1, LANES))


def _merge_kernel(x_ref, g_ref, ysb_ref, o0_ref, o1_ref, o2_ref, l0_ref, l1_ref, l2_ref, ymem_ref,
                  wg_ref, bg_ref, wsb_ref, wdil_ref, wmem_ref, wout_ref, out_ref):
    d = x_ref.shape[-1]
    x = x_ref[...]
    h = _rms(x, g_ref[...]).astype(BF16)
    l0, l1, l2 = l0_ref[...], l1_ref[...], l2_ref[...]
    m = jnp.maximum(jnp.maximum(l0, l1), l2)
    e0, e1, e2 = jnp.exp(l0 - m), jnp.exp(l1 - m), jnp.exp(l2 - m)
    inv = 1.0 / (e0 + e1 + e2)
    y_dil = ((e0 * inv) * o0_ref[...] + (e1 * inv) * o1_ref[...] + (e2 * inv) * o2_ref[...]).astype(BF16)
    merged = jnp.zeros(x.shape, F32)
    branches = ((ysb_ref[...], wsb_ref), (y_dil, wdil_ref), (ymem_ref[...], wmem_ref))
    for n, (y, w_ref) in enumerate(branches):
        pre = jnp.dot(h, wg_ref[:, n * d:(n + 1) * d], preferred_element_type=F32) + bg_ref[:, n * d:(n + 1) * d]
        gate = jax.nn.sigmoid(pre)
        merged = merged + gate * jnp.dot(y, w_ref[...], preferred_element_type=F32)
    out_ref[...] = x + jnp.dot(merged.astype(BF16), wout_ref[...], preferred_element_type=F32)


def _merge(x2d, g_mix, y_sb, o_list, l_list, y_mem, w_gate, b_gate, w_o_sb, w_o_dil, w_o_mem, w_out, tm=256):
    t, d = x2d.shape
    row = lambda w: pl.BlockSpec((tm, w), lambda i: (i, 0))
    full = lambda a: pl.BlockSpec(a.shape, lambda i: (0, 0))
    g2 = g_mix.reshape(1, d)
    b2 = b_gate.reshape(1, -1)
    args = [x2d, g2, y_sb, *o_list, *l_list, y_mem, w_gate, b2, w_o_sb, w_o_dil, w_o_mem, w_out]
    specs = [row(d), full(g2), row(SB_WIDTH)] + [row(DIL_OUT_WIDTH)] * 6 + [row(MEM_WIDTH)] + \
            [full(a) for a in (w_gate, b2, w_o_sb, w_o_dil, w_o_mem, w_out)]
    return pl.pallas_call(
        _merge_kernel,
        grid=(t // tm,),
        in_specs=specs,
        out_specs=row(d),
        out_shape=jax.ShapeDtypeStruct((t, d), F32),
        compiler_params=_cparams(("parallel",)),
        name="merge_project",
    )(*args)


def _oddeven_merge_sort_pairs(n):
    pairs = []

    def merge(lo, hi, r):
        step = r * 2
        if step < hi - lo:
            merge(lo, hi, step)
            merge(lo + r, hi, step)
            for i in range(lo + r, hi - r, step):
                pairs.append((i, i + r))
        else:
            pairs.append((lo, lo + r))

    def sort(lo, hi):
        if hi - lo >= 1:
            mid = lo + (hi - lo) // 2
            sort(lo, mid)
            sort(mid + 1, hi)
            merge(lo, hi, 1)

    sort(0, n - 1)
    return pairs


_SORT16 = _oddeven_merge_sort_pairs(PEER_TOPK)
_CAND = [(a, b) for a in range(PEER_TOPK) for b in range(PEER_TOPK) if (a + 1) * (b + 1) <= PEER_TOPK]


def _top16_sorted(sc):
    n = PEER_KEYS // 8
    v = [sc[8 * i:8 * (i + 1), :] for i in range(n)]
    for (i, j) in _SORT16:
        hi, lo = jnp.maximum(v[i], v[j]), jnp.minimum(v[i], v[j])
        v[i], v[j] = hi, lo
    for shift in (4, 2, 1):
        other = [pltpu.roll(x, shift, 0) for x in v]
        v = [jnp.maximum(v[i], other[n - 1 - i]) for i in range(n)]
        d = n // 2
        while d >= 1:
            for i in range(n):
                if (i // d) % 2 == 0:
                    hi, lo = jnp.maximum(v[i], v[i + d]), jnp.minimum(v[i], v[i + d])
                    v[i], v[i + d] = hi, lo
            d //= 2
    return v


def _peer_select_kernel(x_ref, g_ref, wq_ref, sk_ref, h2_ref, a1_ref, a2_ref, e1_ref, e2_ref, tau_ref,
                        sc_ref, top_ref, cand_ref, invz_ref):
    tm = x_ref.shape[0]
    h2_ref[...] = _rms(x_ref[...], g_ref[...]).astype(BF16)

    def score(hp, _):
        q = jnp.dot(h2_ref[...], wq_ref[hp], preferred_element_type=F32).astype(BF16)
        sc = lax.dot_general(sk_ref[hp], q, (((1,), (1,)), ((), ())), preferred_element_type=F32)
        sc_ref[hp] = sc
        top = _top16_sorted(sc)
        for a in range(PEER_TOPK):
            top_ref[hp % 2, a, pl.ds(hp // 2, 1), :] = top[a][0:1, :]
        return 0

    lax.fori_loop(0, 2 * PEER_HEADS, score, 0)

    for n, (a, b) in enumerate(_CAND):
        cand_ref[n] = top_ref[0, a] + top_ref[1, b]

    def kth_largest(n, tau):
        ci = cand_ref[n]
        cnt = jnp.zeros((8, tm), F32)
        for m in range(len(_CAND)):
            cnt = cnt + jnp.where(cand_ref[m] >= ci, 1.0, 0.0)
        return jnp.maximum(tau, jnp.where(cnt >= float(PEER_TOPK), ci, -jnp.inf))

    tau = lax.fori_loop(0, len(_CAND), kth_largest, jnp.full((8, tm), -jnp.inf, F32))
    tau_ref[...] = tau
    top_sum = cand_ref[0]
    z = jnp.zeros((8, tm), F32)
    for n in range(len(_CAND)):
        ci = cand_ref[n]
        z = z + jnp.where(ci >= tau, jnp.exp(ci - top_sum), 0.0)
    invz_ref[...] = 1.0 / z

    def emit(h, _):
        for p, (a_ref, e_ref) in enumerate(((a1_ref, e1_ref), (a2_ref, e2_ref))):
            sc = sc_ref[2 * h + p]
            kth = top_ref[p, PEER_TOPK - 1, pl.ds(h, 1), :]
            top = top_ref[p, 0, pl.ds(h, 1), :]
            a = jnp.where(sc >= kth, sc, -jnp.inf)
            e = jnp.exp(a - top)
            if p == 0:
                e = e * invz_ref[pl.ds(h, 1), :]
            a_ref[h] = a
            e_ref[h] = e
        return 0

    lax.fori_loop(0, PEER_HEADS, emit, 0)


def _peer_select(x1, g_ffn, w_q, subkeys, tm=256):
    t, d = x1.shape
    halves = 2 * PEER_HEADS
    w_q3 = w_q.reshape(d, halves, PEER_HALF).transpose(1, 0, 2)
    sk3 = subkeys.reshape(halves, PEER_KEYS, PEER_HALF)
    big = jax.ShapeDtypeStruct((PEER_HEADS, PEER_KEYS, t), F32)
    big_spec = pl.BlockSpec((PEER_HEADS, PEER_KEYS, tm), lambda i: (0, 0, i))
    return pl.pallas_call(
        _peer_select_kernel,
        grid=(t // tm,),
        in_specs=[pl.BlockSpec((tm, d), lambda i: (i, 0)),
                  pl.BlockSpec((1, d), lambda i: (0, 0)),
                  pl.BlockSpec(w_q3.shape, lambda i: (0, 0, 0)),
                  pl.BlockSpec(sk3.shape, lambda i: (0, 0, 0))],
        out_specs=[pl.BlockSpec((tm, d), lambda i: (i, 0)), big_spec, big_spec, big_spec, big_spec,
                   pl.BlockSpec((PEER_HEADS, tm), lambda i: (0, i))],
        out_shape=[jax.ShapeDtypeStruct((t, d), BF16), big, big, big, big,
                   jax.ShapeDtypeStruct((PEER_HEADS, t), F32)],
        scratch_shapes=[pltpu.VMEM((halves, PEER_KEYS, tm), F32),
                        pltpu.VMEM((2, PEER_TOPK, 8, tm), F32),
                        pltpu.VMEM((len(_CAND), 8, tm), F32),
                        pltpu.VMEM((8, tm), F32)],
        compiler_params=_cparams(("parallel",)),
        name="peer_select",
    )(x1, g_ffn.reshape(1, d), w_q3, sk3)


def _peer_dense_kernel(x_ref, h2_ref, u_ref, vt_ref, a1_ref, a2_ref, e1_ref, e2_ref, tau_ref,
                       out_ref, acc_ref, w_ref, act_ref, *, rows, chunk):
    j = pl.program_id(1)
    tm = h2_ref.shape[0]

    @pl.when(j == 0)
    def _():
        acc_ref[...] = jnp.zeros(acc_ref.shape, F32)

    act_ref[...] = lax.dot_general(u_ref[...], h2_ref[...], (((1,), (1,)), ((), ())),
                                   preferred_element_type=F32)
    sqrt_half = math.sqrt(0.5)
    group = pl.ds(pl.multiple_of(j * rows, rows), rows)
    for r in range(rows):
        for c in range(tm // chunk):
            tok = slice(c * chunk, (c + 1) * chunk)
            gate = jnp.zeros((PEER_KEYS, chunk), F32)
            for h in range(PEER_HEADS):
                a1 = a1_ref[h, group, tok][r:r + 1, :]
                e1 = e1_ref[h, group, tok][r:r + 1, :]
                s = a1 + a2_ref[h, :, tok]
                gate = gate + jnp.where(s >= tau_ref[h:h + 1, tok], e1 * e2_ref[h, :, tok], 0.0)
            a = act_ref[r * PEER_KEYS:(r + 1) * PEER_KEYS, tok]
            gelu = 0.5 * a * (1.0 + lax.erf(a * sqrt_half))
            w_ref[r * PEER_KEYS:(r + 1) * PEER_KEYS, tok] = (gate * gelu).astype(BF16)
    acc_ref[...] += jnp.dot(vt_ref[...], w_ref[...], preferred_element_type=F32)

    @pl.when(j == pl.num_programs(1) - 1)
    def _():
        out_ref[...] = x_ref[...] + acc_ref[...].T


def _peer_dense(x1, h2, u, vt, a1, a2, e1, e2, tau, tm=512, rows=8, chunk=128):
    t, d = x1.shape
    n_exp = u.shape[0]
    tn = rows * PEER_KEYS
    big_spec = pl.BlockSpec((PEER_HEADS, PEER_KEYS, tm), lambda i, j: (0, 0, i))
    return pl.pallas_call(
        functools.partial(_peer_dense_kernel, rows=rows, chunk=chunk),
        grid=(t // tm, n_exp // tn),
        in_specs=[pl.BlockSpec((tm, d), lambda i, j: (i, 0)),
                  pl.BlockSpec((tm, d), lambda i, j: (i, 0)),
                  pl.BlockSpec((tn, d), lambda i, j: (j, 0)),
                  pl.BlockSpec((d, tn), lambda i, j: (0, j)),
                  big_spec, big_spec, big_spec, big_spec,
                  pl.BlockSpec((PEER_HEADS, tm), lambda i, j: (0, i))],
        out_specs=pl.BlockSpec((tm, d), lambda i, j: (i, 0)),
        out_shape=jax.ShapeDtypeStruct((t, d), F32),
        scratch_shapes=[pltpu.VMEM((d, tm), F32), pltpu.VMEM((tn, tm), BF16), pltpu.VMEM((tn, tm), F32)],
        compiler_params=_cparams(("parallel", "arbitrary")),
        name="peer_dense",
    )(x1, h2, u, vt, a1, a2, e1, e2, tau)


def kernel(x, mem, g_mix, g_mem, w_in, w_mem_kv, g_q_dil, g_k_dil, g_q_mem, g_k_mem, w_o_sb, w_o_dil,
           w_o_mem, w_gate, b_gate, w_out, g_ffn, w_peer_q, peer_subkeys, peer_u, peer_v):
    batch, seq, d = x.shape
    mem_len = mem.shape[1]
    depth = w_in.shape[0]
    xt = x.reshape(batch * seq, d)
    for l in range(depth):
        w_in_l = w_in[l].astype(BF16)
        c_sb, c_dil = 3 * SB_WIDTH, 3 * SB_WIDTH + 3 * DIL_WIDTH
        sb = _norm_matmul(xt, g_mix[l], w_in_l[:, :c_sb], BF16, 512, 512)
        dil = _norm_matmul(xt, g_mix[l], w_in_l[:, c_sb:c_dil], F32, 512, 768)
        mq = _norm_matmul(xt, g_mix[l], w_in_l[:, c_dil:], F32, 512, 512)
        kv = _norm_matmul(mem.reshape(batch * mem_len, d), g_mem[l], w_mem_kv[l].astype(BF16), F32, 512, 512)

        y_sb = _sb_attention(sb.reshape(batch, seq, c_sb), batch, seq)

        g_q2 = jnp.tile(g_q_dil[l], 2).reshape(1, LANES)
        g_k2 = jnp.tile(g_k_dil[l], 2).reshape(1, LANES)
        dil3 = dil.reshape(batch, seq, 3 * DIL_WIDTH)
        o_list, l_list = [], []
        for group in range(len(DIL_CONFIG)):
            o, lse = _dil_group(dil3, g_q2, g_k2, group, batch, seq)
            o_list.append(o.reshape(batch * seq, DIL_OUT_WIDTH))
            l_list.append(lse.reshape(batch * seq, DIL_OUT_WIDTH))

        y_mem = _mem_attention(mq.reshape(batch, seq, MEM_WIDTH), kv.reshape(batch, mem_len, 2 * MEM_WIDTH),
                               g_q_mem[l], g_k_mem[l], batch, seq, mem_len)

        x1 = _merge(xt, g_mix[l], y_sb.reshape(batch * seq, SB_WIDTH), o_list, l_list,
                    y_mem.reshape(batch * seq, MEM_WIDTH), w_gate[l].astype(BF16), b_gate[l],
                    w_o_sb[l].astype(BF16), w_o_dil[l].astype(BF16), w_o_mem[l].astype(BF16),
                    w_out[l].astype(BF16))

        h2, a1, a2, e1, e2, tau = _peer_select(x1, g_ffn[l], w_peer_q[l].astype(BF16),
                                               peer_subkeys[l].astype(BF16))
        xt = _peer_dense(x1, h2, peer_u[l].astype(BF16), peer_v[l].astype(BF16).T, a1, a2, e1, e2, tau)
    return xt.reshape(batch, seq, d)
```

```python
import functools
import math

import numpy as np
import jax
import jax.numpy as jnp
from jax import lax
from jax.experimental import pallas as pl
from jax.experimental.pallas import tpu as pltpu

F32 = jnp.float32
BF16 = jnp.bfloat16

EPS = 1e-6
HEAD_DIM = 64
LANES = 128
SB_HEADS = 8
SB_WIDTH = SB_HEADS * HEAD_DIM
DIL_CONFIG = ((128, 1), (512, 4), (2048, 16))
DIL_HEADS_PER_GROUP = 4
DIL_HEADS = len(DIL_CONFIG) * DIL_HEADS_PER_GROUP
DIL_WIDTH = DIL_HEADS * HEAD_DIM
DIL_OUT_WIDTH = DIL_HEADS_PER_GROUP * HEAD_DIM
DIL_BLOCK = 128
ALIBI_MAX_BIAS = 8.0
MEM_HEADS = 4
MEM_HEAD_DIM = 128
MEM_WIDTH = MEM_HEADS * MEM_HEAD_DIM
PEER_HEADS = 8
PEER_KEYS = 128
PEER_TOPK = 16
PEER_HALF = 128
NEG_BIG = -1e30
VMEM_LIMIT = 56 * 1024 * 1024


def _cparams(sem):
    return pltpu.CompilerParams(dimension_semantics=sem, vmem_limit_bytes=VMEM_LIMIT)


def _rms(x, g):
    ms = jnp.mean(x * x, axis=-1, keepdims=True)
    return x * lax.rsqrt(ms + EPS) * g


def _norm_matmul_kernel(x_ref, g_ref, w_ref, o_ref, h_ref):
    @pl.when(pl.program_id(1) == 0)
    def _():
        h_ref[...] = _rms(x_ref[...], g_ref[...]).astype(BF16)

    o_ref[...] = jnp.dot(h_ref[...], w_ref[...], preferred_element_type=F32).astype(o_ref.dtype)


def _norm_matmul(x2d, g, w, out_dtype, tm, tn):
    m, k = x2d.shape
    n = w.shape[1]
    return pl.pallas_call(
        _norm_matmul_kernel,
        grid=(m // tm, n // tn),
        in_specs=[pl.BlockSpec((tm, k), lambda i, j: (i, 0)),
                  pl.BlockSpec((1, k), lambda i, j: (0, 0)),
                  pl.BlockSpec((k, tn), lambda i, j: (0, j))],
        out_specs=pl.BlockSpec((tm, tn), lambda i, j: (i, j)),
        out_shape=jax.ShapeDtypeStruct((m, n), out_dtype),
        scratch_shapes=[pltpu.VMEM((tm, k), BF16)],
        compiler_params=_cparams(("parallel", "arbitrary")),
        name="norm_matmul",
    )(x2d, g.reshape(1, k), w)


def _sb_kernel(q_ref, k_ref, v_ref, o_ref, acc_ref, *, tq):
    qi = pl.program_id(2)
    lane = lax.broadcasted_iota(jnp.int32, (tq, LANES), 1)
    row = lax.broadcasted_iota(jnp.int32, (tq, tq), 0)
    col = lax.broadcasted_iota(jnp.int32, (tq, tq), 1)
    later = (row > col).astype(BF16)
    past = col < row
    q2 = q_ref[...] * (HEAD_DIM ** -0.5)
    zero = jnp.zeros_like(q2)
    qm = (jnp.where(lane < HEAD_DIM, q2, zero), jnp.where(lane < HEAD_DIM, zero, q2))

    def block(j, newer, diagonal):
        start = pl.multiple_of(j * tq, tq)
        kb = k_ref[pl.ds(start, tq), :]
        vb = v_ref[pl.ds(start, tq), :]
        out = []
        for half in range(2):
            z = lax.dot_general(qm[half], kb, (((1,), (1,)), ((), ())), preferred_element_type=F32)
            sp = jnp.maximum(z, 0.0) + jnp.log1p(jnp.exp(-jnp.abs(z)))
            log_keep = jnp.where(past, -sp, 0.0) if diagonal else -sp
            local = jnp.dot(log_keep.astype(BF16), later, preferred_element_type=F32)
            a = jnp.exp((z - sp) + (local + newer[half]))
            if diagonal:
                a = jnp.where(past, a, 0.0)
            pv = jnp.dot(a.astype(BF16), vb, preferred_element_type=F32)
            if diagonal:
                acc_ref[half] = pv
            else:
                acc_ref[half] += pv
            out.append(newer[half] + jnp.sum(log_keep, axis=-1, keepdims=True))
        return tuple(out)

    zeros = jnp.zeros((tq, 1), F32)
    newer = block(qi, (zeros, zeros), True)
    lax.fori_loop(0, qi, lambda jj, nw: block(qi - 1 - jj, nw, False), newer)
    o_ref[...] = jnp.where(lane < HEAD_DIM, acc_ref[0], acc_ref[1]).astype(o_ref.dtype)


def _sb_attention(sb, batch, seq, tq=256):
    pairs = SB_WIDTH // LANES
    return pl.pallas_call(
        functools.partial(_sb_kernel, tq=tq),
        grid=(batch, pairs, seq // tq),
        in_specs=[pl.BlockSpec((None, tq, LANES), lambda b, p, i: (b, i, p)),
                  pl.BlockSpec((None, seq, LANES), lambda b, p, i: (b, 0, pairs + p)),
                  pl.BlockSpec((None, seq, LANES), lambda b, p, i: (b, 0, 2 * pairs + p))],
        out_specs=pl.BlockSpec((None, tq, LANES), lambda b, p, i: (b, i, p)),
        out_shape=jax.ShapeDtypeStruct((batch, seq, SB_WIDTH), BF16),
        scratch_shapes=[pltpu.VMEM((2, tq, LANES), F32)],
        compiler_params=_cparams(("parallel", "parallel", "parallel")),
        name="sb_attention",
    )(sb, sb, sb)


def _head_rms(x, g, lane):
    xx = x * x
    lo = lane < HEAD_DIM
    s_lo = jnp.sum(jnp.where(lo, xx, 0.0), axis=-1, keepdims=True)
    s_hi = jnp.sum(jnp.where(lo, 0.0, xx), axis=-1, keepdims=True)
    ms = jnp.where(lo, s_lo, s_hi) * (1.0 / HEAD_DIM)
    return x * lax.rsqrt(ms + EPS) * g


def _dil_kernel(q_ref, k_ref, v_ref, gq_ref, gk_ref, o_ref, l_ref, qn_ref, kn_ref, vn_ref,
                *, dilation, length, slopes, scale):
    nb = length // DIL_BLOCK
    pair = pl.program_id(1)
    lane_l = lax.broadcasted_iota(jnp.int32, (length, LANES), 1)
    lane_b = lax.broadcasted_iota(jnp.int32, (DIL_BLOCK, LANES), 1)
    qi = lax.broadcasted_iota(jnp.int32, (DIL_BLOCK, 2 * DIL_BLOCK), 0)
    kj = lax.broadcasted_iota(jnp.int32, (DIL_BLOCK, 2 * DIL_BLOCK), 1) - DIL_BLOCK
    gap = qi - kj
    band = (gap >= 0) & (gap <= DIL_BLOCK)
    gapf = (gap * dilation).astype(F32)
    kn_ref[pl.ds(0, DIL_BLOCK), :] = jnp.zeros((DIL_BLOCK, LANES), BF16)
    vn_ref[pl.ds(0, DIL_BLOCK), :] = jnp.zeros((DIL_BLOCK, LANES), BF16)

    def stream(c, _):
        if dilation == 1:
            rows = pl.ds(0, length)
        else:
            rows = pl.ds(c, length, stride=dilation)
        qn_ref[...] = _head_rms(q_ref[rows, :], gq_ref[...], lane_l).astype(BF16)
        kn_ref[pl.ds(DIL_BLOCK, length), :] = _head_rms(k_ref[rows, :], gk_ref[...], lane_l).astype(BF16)
        vn_ref[pl.ds(DIL_BLOCK, length), :] = v_ref[rows, :].astype(BF16)

        def block(n, _):
            start = pl.multiple_of(n * DIL_BLOCK, DIL_BLOCK)
            qb = qn_ref[pl.ds(start, DIL_BLOCK), :]
            kw = kn_ref[pl.ds(start, 2 * DIL_BLOCK), :]
            vw = vn_ref[pl.ds(start, 2 * DIL_BLOCK), :]
            valid = band & ((n * DIL_BLOCK + kj) >= 0)
            outs, lses = [], []
            for half in range(2):
                in_head = (lane_b < HEAD_DIM) if half == 0 else (lane_b >= HEAD_DIM)
                qm = jnp.where(in_head, qb, jnp.zeros_like(qb))
                slope = jnp.where(pair == 0, slopes[half], slopes[2 + half])
                s = lax.dot_general(qm, kw, (((1,), (1,)), ((), ())),
                                    preferred_element_type=F32) * scale
                s = jnp.where(valid, s - slope * gapf, NEG_BIG)
                m = jnp.max(s, axis=-1, keepdims=True)
                e = jnp.exp(s - m)
                denom = jnp.sum(e, axis=-1, keepdims=True)
                o = jnp.dot(e.astype(BF16), vw, preferred_element_type=F32) / denom
                outs.append(o)
                lses.append(m + jnp.log(denom))
            lo = lane_b < HEAD_DIM
            o2 = jnp.where(lo, outs[0], outs[1])
            l2 = jnp.where(lo, lses[0], lses[1])
            if dilation == 1:
                dst = pl.ds(start, DIL_BLOCK)
            else:
                dst = pl.ds(c + start * dilation, DIL_BLOCK, stride=dilation)
            o_ref[dst, :] = o2
            l_ref[dst, :] = l2
            return 0

        lax.fori_loop(0, nb, block, 0)
        return 0

    lax.fori_loop(0, dilation, stream, 0)


def _dil_group(dil, g_q2, g_k2, group, batch, seq):
    _, dilation = DIL_CONFIG[group]
    length = seq // dilation
    blocks = DIL_WIDTH // LANES
    pairs = DIL_OUT_WIDTH // LANES
    all_slopes = [2.0 ** (-ALIBI_MAX_BIAS * (i + 1) / DIL_HEADS) for i in range(DIL_HEADS)]
    slopes = tuple(float(np.float32(s)) for s in
                   all_slopes[group * DIL_HEADS_PER_GROUP:(group + 1) * DIL_HEADS_PER_GROUP])
    kern = functools.partial(_dil_kernel, dilation=dilation, length=length, slopes=slopes,
                             scale=HEAD_DIM ** -0.5)
    col = lambda off: (lambda b, p: (b, 0, off + group * pairs + p))
    out_sd = jax.ShapeDtypeStruct((batch, seq, DIL_OUT_WIDTH), F32)
    return pl.pallas_call(
        kern,
        grid=(batch, pairs),
        in_specs=[pl.BlockSpec((None, seq, LANES), col(0)),
                  pl.BlockSpec((None, seq, LANES), col(blocks)),
                  pl.BlockSpec((None, seq, LANES), col(2 * blocks)),
                  pl.BlockSpec((1, LANES), lambda b, p: (0, 0)),
                  pl.BlockSpec((1, LANES), lambda b, p: (0, 0))],
        out_specs=[pl.BlockSpec((None, seq, LANES), lambda b, p: (b, 0, p)),
                   pl.BlockSpec((None, seq, LANES), lambda b, p: (b, 0, p))],
        out_shape=[out_sd, out_sd],
        scratch_shapes=[pltpu.VMEM((length, LANES), BF16),
                        pltpu.VMEM((length + DIL_BLOCK, LANES), BF16),
                        pltpu.VMEM((length + DIL_BLOCK, LANES), BF16)],
        compiler_params=_cparams(("parallel", "parallel")),
        name=f"dilated_attention_r{dilation}",
    )(dil, dil, dil, g_q2, g_k2)


def _mem_kernel(q_ref, k_ref, v_ref, gq_ref, gk_ref, o_ref, *, scale):
    q = _rms(q_ref[...], gq_ref[...]).astype(BF16)
    k = _rms(k_ref[...], gk_ref[...]).astype(BF16)
    s = lax.dot_general(q, k, (((1,), (1,)), ((), ())), preferred_element_type=F32) * scale
    m = jnp.max(s, axis=-1, keepdims=True)
    e = jnp.exp(s - m)
    denom = jnp.sum(e, axis=-1, keepdims=True)
    o = jnp.dot(e.astype(BF16), v_ref[...].astype(BF16), preferred_element_type=F32) / denom
    o_ref[...] = o.astype(o_ref.dtype)


def _mem_attention(mq, kv, g_q, g_k, batch, seq, mem_len, tq=512):
    return pl.pallas_call(
        functools.partial(_mem_kernel, scale=MEM_HEAD_DIM ** -0.5),
        grid=(batch, MEM_HEADS, seq // tq),
        in_specs=[pl.BlockSpec((None, tq, LANES), lambda b, h, i: (b, i, h)),
                  pl.BlockSpec((None, mem_len, LANES), lambda b, h, i: (b, 0, h)),
                  pl.BlockSpec((None, mem_len, LANES), lambda b, h, i: (b, 0, MEM_HEADS + h)),
                  pl.BlockSpec((1, LANES), lambda b, h, i: (0, 0)),
                  pl.BlockSpec((1, LANES), lambda b, h, i: (0, 0))],
        out_specs=pl.BlockSpec((None, tq, LANES), lambda b, h, i: (b, i, h)),
        out_shape=jax.ShapeDtypeStruct((batch, seq, MEM_WIDTH), BF16),
        compiler_params=_cparams(("parallel", "parallel", "parallel")),
        name="memory_attention",
    )(mq, kv, kv, g_q.reshape(1, LANES), g_k.reshape(1, LANES))


def _merge_kernel(x_ref, g_ref, ysb_ref, o0_ref, o1_ref, o2_ref, l0_ref, l1_ref, l2_ref, ymem_ref,
                  wg_ref, bg_ref, wsb_ref, wdil_ref, wmem_ref, wout_ref, out_ref):
    d = x_ref.shape[-1]
    x = x_ref[...]
    h = _rms(x, g_ref[...]).astype(BF16)
    l0, l1, l2 = l0_ref[...], l1_ref[...], l2_ref[...]
    m = jnp.maximum(jnp.maximum(l0, l1), l2)
    e0, e1, e2 = jnp.exp(l0 - m), jnp.exp(l1 - m), jnp.exp(l2 - m)
    inv = 1.0 / (e0 + e1 + e2)
    y_dil = ((e0 * inv) * o0_ref[...] + (e1 * inv) * o1_ref[...] + (e2 * inv) * o2_ref[...]).astype(BF16)
    merged = jnp.zeros(x.shape, F32)
    branches = ((ysb_ref[...], wsb_ref), (y_dil, wdil_ref), (ymem_ref[...], wmem_ref))
    for n, (y, w_ref) in enumerate(branches):
        pre = jnp.dot(h, wg_ref[:, n * d:(n + 1) * d], preferred_element_type=F32) + bg_ref[:, n * d:(n + 1) * d]
        gate = jax.nn.sigmoid(pre)
        merged = merged + gate * jnp.dot(y, w_ref[...], preferred_element_type=F32)
    out_ref[...] = x + jnp.dot(merged.astype(BF16), wout_ref[...], preferred_element_type=F32)


def _merge(x2d, g_mix, y_sb, o_list, l_list, y_mem, w_gate, b_gate, w_o_sb, w_o_dil, w_o_mem, w_out, tm=256):
    t, d = x2d.shape
    row = lambda w: pl.BlockSpec((tm, w), lambda i: (i, 0))
    full = lambda a: pl.BlockSpec(a.shape, lambda i: (0, 0))
    g2 = g_mix.reshape(1, d)
    b2 = b_gate.reshape(1, -1)
    args = [x2d, g2, y_sb, *o_list, *l_list, y_mem, w_gate, b2, w_o_sb, w_o_dil, w_o_mem, w_out]
    specs = [row(d), full(g2), row(SB_WIDTH)] + [row(DIL_OUT_WIDTH)] * 6 + [row(MEM_WIDTH)] + \
            [full(a) for a in (w_gate, b2, w_o_sb, w_o_dil, w_o_mem, w_out)]
    return pl.pallas_call(
        _merge_kernel,
        grid=(t // tm,),
        in_specs=specs,
        out_specs=row(d),
        out_shape=jax.ShapeDtypeStruct((t, d), F32),
        compiler_params=_cparams(("parallel",)),
        name="merge_project",
    )(*args)


def _oddeven_merge_sort_pairs(n):
    pairs = []

    def merge(lo, hi, r):
        step = r * 2
        if step < hi - lo:
            merge(lo, hi, step)
            merge(lo + r, hi, step)
            for i in range(lo + r, hi - r, step):
                pairs.append((i, i + r))
        else:
            pairs.append((lo, lo + r))

    def sort(lo, hi):
        if hi - lo >= 1:
            mid = lo + (hi - lo) // 2
            sort(lo, mid)
            sort(mid + 1, hi)
            merge(lo, hi, 1)

    sort(0, n - 1)
    return pairs


_SORT16 = _oddeven_merge_sort_pairs(PEER_TOPK)
_CAND = [(a, b) for a in range(PEER_TOPK) for b in range(PEER_TOPK) if (a + 1) * (b + 1) <= PEER_TOPK]


def _top16_sorted(sc):
    n = PEER_KEYS // 8
    v = [sc[8 * i:8 * (i + 1), :] for i in range(n)]
    for (i, j) in _SORT16:
        hi, lo = jnp.maximum(v[i], v[j]), jnp.minimum(v[i], v[j])
        v[i], v[j] = hi, lo
    for shift in (4, 2, 1):
        other = [pltpu.roll(x, shift, 0) for x in v]
        v = [jnp.maximum(v[i], other[n - 1 - i]) for i in range(n)]
        d = n // 2
        while d >= 1:
            for i in range(n):
                if (i // d) % 2 == 0:
                    hi, lo = jnp.maximum(v[i], v[i + d]), jnp.minimum(v[i], v[i + d])
                    v[i], v[i + d] = hi, lo
            d //= 2
    return v


def _peer_select_kernel(x_ref, g_ref, wq_ref, sk_ref, h2_ref, c1_ref, e1_ref, r2_ref, e2_ref,
                        q_ref, sc_ref, top_ref, cand_ref, tau_ref, invz_ref):
    tm = x_ref.shape[0]
    h2 = _rms(x_ref[...], g_ref[...]).astype(BF16)
    h2_ref[...] = h2
    q_ref[...] = jnp.dot(h2, wq_ref[...], preferred_element_type=F32).astype(BF16)
    for hp in range(2 * PEER_HEADS):
        sc_ref[hp] = lax.dot_general(sk_ref[hp], q_ref[:, hp * PEER_HALF:(hp + 1) * PEER_HALF],
                                     (((1,), (1,)), ((), ())), preferred_element_type=F32)

    def sort_head(h, _):
        for p in range(2):
            top = _top16_sorted(sc_ref[2 * h + p])
            for a in range(PEER_TOPK):
                top_ref[p, a, pl.ds(h, 1), :] = top[a][0:1, :]
        return 0

    lax.fori_loop(0, PEER_HEADS, sort_head, 0)

    for n, (a, b) in enumerate(_CAND):
        cand_ref[n] = top_ref[0, a] + top_ref[1, b]

    def kth_largest(n, tau):
        ci = cand_ref[n]
        cnt = jnp.zeros((8, tm), F32)
        for m in range(len(_CAND)):
            cnt = cnt + jnp.where(cand_ref[m] >= ci, 1.0, 0.0)
        return jnp.maximum(tau, jnp.where(cnt >= float(PEER_TOPK), ci, -jnp.inf))

    tau = lax.fori_loop(0, len(_CAND), kth_largest, jnp.full((8, tm), -jnp.inf, F32))
    tau_ref[...] = tau
    top_sum = cand_ref[0]
    z = jnp.zeros((8, tm), F32)
    for n in range(len(_CAND)):
        ci = cand_ref[n]
        z = z + jnp.where(ci >= tau, jnp.exp(ci - top_sum), 0.0)
    invz_ref[...] = 0.5 / z

    def emit(h, _):
        row = pl.ds(h, 1)
        s1, s2 = sc_ref[2 * h], sc_ref[2 * h + 1]
        tau_h = tau_ref[row, :]
        cnt = jnp.zeros(s1.shape, F32)
        rank = jnp.zeros(s2.shape, F32)
        for b in range(PEER_TOPK):
            s2b = top_ref[1, b, row, :]
            cnt = cnt + jnp.where((s1 + s2b) >= tau_h, 1.0, 0.0)
            rank = rank + jnp.where(s2b > s2, 1.0, 0.0)
        in_top = s1 >= top_ref[0, PEER_TOPK - 1, row, :]
        c1_ref[h] = jnp.where(in_top, cnt, 0.0)
        e1_ref[h] = jnp.exp(s1 - top_ref[0, 0, row, :]) * invz_ref[row, :]
        r2_ref[h] = rank.astype(BF16)
        e2_ref[h] = jnp.exp(s2 - top_ref[1, 0, row, :]).astype(BF16)
        return 0

    lax.fori_loop(0, PEER_HEADS, emit, 0)


def _peer_select(x1, g_ffn, w_q, subkeys, tm=512):
    t, d = x1.shape
    halves = 2 * PEER_HEADS
    sk3 = subkeys.reshape(halves, PEER_KEYS, PEER_HALF)
    shape = (PEER_HEADS, PEER_KEYS, t)
    big_spec = pl.BlockSpec((PEER_HEADS, PEER_KEYS, tm), lambda i: (0, 0, i))
    return pl.pallas_call(
        _peer_select_kernel,
        grid=(t // tm,),
        in_specs=[pl.BlockSpec((tm, d), lambda i: (i, 0)),
                  pl.BlockSpec((1, d), lambda i: (0, 0)),
                  pl.BlockSpec(w_q.shape, lambda i: (0, 0)),
                  pl.BlockSpec(sk3.shape, lambda i: (0, 0, 0))],
        out_specs=[pl.BlockSpec((tm, d), lambda i: (i, 0)), big_spec, big_spec, big_spec, big_spec],
        out_shape=[jax.ShapeDtypeStruct((t, d), BF16),
                   jax.ShapeDtypeStruct(shape, F32), jax.ShapeDtypeStruct(shape, F32),
                   jax.ShapeDtypeStruct(shape, BF16), jax.ShapeDtypeStruct(shape, BF16)],
        scratch_shapes=[pltpu.VMEM((tm, halves * PEER_HALF), BF16),
                        pltpu.VMEM((halves, PEER_KEYS, tm), F32),
                        pltpu.VMEM((2, PEER_TOPK, 8, tm), F32),
                        pltpu.VMEM((len(_CAND), 8, tm), F32),
                        pltpu.VMEM((8, tm), F32),
                        pltpu.VMEM((8, tm), F32)],
        compiler_params=_cparams(("parallel",)),
        name="peer_select",
    )(x1, g_ffn.reshape(1, d), w_q, sk3)


PACK = 16


def _peer_dense_kernel(x_ref, h2_ref, u_ref, vt_ref, c1_ref, e1_ref, r2_in_ref, e2_in_ref,
                       out_ref, acc_ref, w_ref, act_ref, r2_ref, e2_ref, *, rows, chunk):
    j = pl.program_id(1)
    tm = h2_ref.shape[0]

    @pl.when(j == 0)
    def _():
        acc_ref[...] = jnp.zeros(acc_ref.shape, F32)
        r2_ref[...] = r2_in_ref[...]
        e2_ref[...] = e2_in_ref[...]

    act_ref[...] = lax.dot_general(u_ref[...], h2_ref[...], (((1,), (1,)), ((), ())),
                                   preferred_element_type=F32)
    sqrt_half = math.sqrt(0.5)
    group = pl.ds(pl.multiple_of(j * rows, rows), rows)
    n_pack = PEER_KEYS // PACK
    for c in range(tm // chunk):
        tok = slice(c * chunk, (c + 1) * chunk)
        for r0 in range(0, rows, 2):
            gate = [[jnp.zeros((PACK, chunk), BF16) for _ in range(n_pack)] for _ in range(2)]
            for h in range(PEER_HEADS):
                c1g = c1_ref[h, group, tok]
                e1g = e1_ref[h, group, tok]
                c1 = [jnp.broadcast_to(c1g[r0 + k:r0 + k + 1, :], (PACK, chunk)).astype(BF16) for k in range(2)]
                e1 = [jnp.broadcast_to(e1g[r0 + k:r0 + k + 1, :], (PACK, chunk)).astype(BF16) for k in range(2)]
                for m in range(n_pack):
                    r2 = r2_ref[h, m * PACK:(m + 1) * PACK, tok]
                    e2 = e2_ref[h, m * PACK:(m + 1) * PACK, tok]
                    for k in range(2):
                        gate[k][m] = gate[k][m] + jnp.where(r2 < c1[k], e2 * e1[k], jnp.zeros_like(e2))
            for k in range(2):
                base = (r0 + k) * PEER_KEYS
                for m in range(n_pack):
                    a = act_ref[base + m * PACK:base + (m + 1) * PACK, tok]
                    gelu2 = a * (1.0 + lax.erf(a * sqrt_half))
                    w_ref[base + m * PACK:base + (m + 1) * PACK, tok] = gate[k][m] * gelu2.astype(BF16)
    acc_ref[...] += jnp.dot(vt_ref[...], w_ref[...], preferred_element_type=F32)

    @pl.when(j == pl.num_programs(1) - 1)
    def _():
        out_ref[...] = x_ref[...] + acc_ref[...].T


def _peer_dense(x1, h2, u, vt, c1, e1, r2, e2, tm=512, rows=8, chunk=128):
    t, d = x1.shape
    n_exp = u.shape[0]
    tn = rows * PEER_KEYS
    big_spec = pl.BlockSpec((PEER_HEADS, PEER_KEYS, tm), lambda i, j: (0, 0, i))
    return pl.pallas_call(
        functools.partial(_peer_dense_kernel, rows=rows, chunk=chunk),
        grid=(t // tm, n_exp // tn),
        in_specs=[pl.BlockSpec((tm, d), lambda i, j: (i, 0)),
                  pl.BlockSpec((tm, d), lambda i, j: (i, 0)),
                  pl.BlockSpec((tn, d), lambda i, j: (j, 0)),
                  pl.BlockSpec((d, tn), lambda i, j: (0, j)),
                  big_spec, big_spec, big_spec, big_spec],
        out_specs=pl.BlockSpec((tm, d), lambda i, j: (i, 0)),
        out_shape=jax.ShapeDtypeStruct((t, d), F32),
        scratch_shapes=[pltpu.VMEM((d, tm), F32), pltpu.VMEM((tn, tm), BF16), pltpu.VMEM((tn, tm), F32),
                        pltpu.VMEM((PEER_HEADS, PEER_KEYS, tm), BF16),
                        pltpu.VMEM((PEER_HEADS, PEER_KEYS, tm), BF16)],
        compiler_params=_cparams(("parallel", "arbitrary")),
        name="peer_dense",
    )(x1, h2, u, vt, c1, e1, r2, e2)


def kernel(x, mem, g_mix, g_mem, w_in, w_mem_kv, g_q_dil, g_k_dil, g_q_mem, g_k_mem, w_o_sb, w_o_dil,
           w_o_mem, w_gate, b_gate, w_out, g_ffn, w_peer_q, peer_subkeys, peer_u, peer_v):
    batch, seq, d = x.shape
    mem_len = mem.shape[1]
    depth = w_in.shape[0]
    xt = x.reshape(batch * seq, d)
    for l in range(depth):
        w_in_l = w_in[l].astype(BF16)
        c_sb, c_dil = 3 * SB_WIDTH, 3 * SB_WIDTH + 3 * DIL_WIDTH
        sb = _norm_matmul(xt, g_mix[l], w_in_l[:, :c_sb], BF16, 512, 512)
        dil = _norm_matmul(xt, g_mix[l], w_in_l[:, c_sb:c_dil], F32, 512, 768)
        mq = _norm_matmul(xt, g_mix[l], w_in_l[:, c_dil:], F32, 512, 512)
        kv = _norm_matmul(mem.reshape(batch * mem_len, d), g_mem[l], w_mem_kv[l].astype(BF16), F32, 512, 512)

        y_sb = _sb_attention(sb.reshape(batch, seq, c_sb), batch, seq)

        g_q2 = jnp.tile(g_q_dil[l], 2).reshape(1, LANES)
        g_k2 = jnp.tile(g_k_dil[l], 2).reshape(1, LANES)
        dil3 = dil.reshape(batch, seq, 3 * DIL_WIDTH)
        o_list, l_list = [], []
        for group in range(len(DIL_CONFIG)):
            o, lse = _dil_group(dil3, g_q2, g_k2, group, batch, seq)
            o_list.append(o.reshape(batch * seq, DIL_OUT_WIDTH))
            l_list.append(lse.reshape(batch * seq, DIL_OUT_WIDTH))

        y_mem = _mem_attention(mq.reshape(batch, seq, MEM_WIDTH), kv.reshape(batch, mem_len, 2 * MEM_WIDTH),
                               g_q_mem[l], g_k_mem[l], batch, seq, mem_len)

        x1 = _merge(xt, g_mix[l], y_sb.reshape(batch * seq, SB_WIDTH), o_list, l_list,
                    y_mem.reshape(batch * seq, MEM_WIDTH), w_gate[l].astype(BF16), b_gate[l],
                    w_o_sb[l].astype(BF16), w_o_dil[l].astype(BF16), w_o_mem[l].astype(BF16),
                    w_out[l].astype(BF16))

        h2, c1, e1, r2, e2 = _peer_select(x1, g_ffn[l], w_peer_q[l].astype(BF16),
                                          peer_subkeys[l].astype(BF16))
        xt = _peer_dense(x1, h2, peer_u[l].astype(BF16), peer_v[l].astype(BF16).T, c1, e1, r2, e2)
    return xt.reshape(batch, seq, d)
```

```python
import functools
import math

import numpy as np
import jax
import jax.numpy as jnp
from jax import lax
from jax.experimental import pallas as pl
from jax.experimental.pallas import tpu as pltpu

F32 = jnp.float32
BF16 = jnp.bfloat16

EPS = 1e-6
HEAD_DIM = 64
LANES = 128
SB_HEADS = 8
SB_WIDTH = SB_HEADS * HEAD_DIM
DIL_CONFIG = ((128, 1), (512, 4), (2048, 16))
DIL_HEADS_PER_GROUP = 4
DIL_HEADS = len(DIL_CONFIG) * DIL_HEADS_PER_GROUP
DIL_WIDTH = DIL_HEADS * HEAD_DIM
DIL_OUT_WIDTH = DIL_HEADS_PER_GROUP * HEAD_DIM
DIL_BLOCK = 128
ALIBI_MAX_BIAS = 8.0
MEM_HEADS = 4
MEM_HEAD_DIM = 128
MEM_WIDTH = MEM_HEADS * MEM_HEAD_DIM
PEER_HEADS = 8
PEER_KEYS = 128
PEER_TOPK = 16
PEER_HALF = 128
NEG_BIG = -1e30
LOG2E = 1.4426950408889634
VMEM_LIMIT = 56 * 1024 * 1024


def _cparams(sem, flags=None):
    return pltpu.CompilerParams(dimension_semantics=sem, vmem_limit_bytes=VMEM_LIMIT, flags=flags)


def _rms(x, g):
    ms = jnp.mean(x * x, axis=-1, keepdims=True)
    return x * lax.rsqrt(ms + EPS) * g


def _norm_matmul_kernel(x_ref, g_ref, w_ref, o_ref, h_ref):
    @pl.when(pl.program_id(1) == 0)
    def _():
        h_ref[...] = _rms(x_ref[...], g_ref[...]).astype(BF16)

    o_ref[...] = jnp.dot(h_ref[...], w_ref[...], preferred_element_type=F32).astype(o_ref.dtype)


def _norm_matmul(x2d, g, w, out_dtype, tm, tn):
    m, k = x2d.shape
    n = w.shape[1]
    return pl.pallas_call(
        _norm_matmul_kernel,
        grid=(m // tm, n // tn),
        in_specs=[pl.BlockSpec((tm, k), lambda i, j: (i, 0)),
                  pl.BlockSpec((1, k), lambda i, j: (0, 0)),
                  pl.BlockSpec((k, tn), lambda i, j: (0, j))],
        out_specs=pl.BlockSpec((tm, tn), lambda i, j: (i, j)),
        out_shape=jax.ShapeDtypeStruct((m, n), out_dtype),
        scratch_shapes=[pltpu.VMEM((tm, k), BF16)],
        compiler_params=_cparams(("parallel", "arbitrary")),
        name="norm_matmul",
    )(x2d, g.reshape(1, k), w)


def _in_proj_kernel(x_ref, g_ref, w_ref, *out_refs, tn):
    h = _rms(x_ref[...], g_ref[...]).astype(BF16)
    col = 0
    for o_ref in out_refs:
        width = o_ref.shape[1]
        for c0 in range(0, width, tn):
            c1 = min(c0 + tn, width)
            o_ref[:, c0:c1] = jnp.dot(h, w_ref[:, col + c0:col + c1],
                                      preferred_element_type=F32).astype(o_ref.dtype)
        col += width


def _in_proj(x2d, g, w, widths, dtypes, tm=512, tn=512):
    m, k = x2d.shape
    return pl.pallas_call(
        functools.partial(_in_proj_kernel, tn=tn),
        grid=(m // tm,),
        in_specs=[pl.BlockSpec((tm, k), lambda i: (i, 0)),
                  pl.BlockSpec((1, k), lambda i: (0, 0)),
                  pl.BlockSpec(w.shape, lambda i: (0, 0))],
        out_specs=[pl.BlockSpec((tm, n), lambda i: (i, 0)) for n in widths],
        out_shape=[jax.ShapeDtypeStruct((m, n), dt) for n, dt in zip(widths, dtypes)],
        compiler_params=_cparams(("parallel",)),
        name="in_proj",
    )(x2d, g.reshape(1, k), w)


def _sb_kernel(q_ref, k_ref, v_ref, o_ref, acc_ref, *, tq):
    qi = pl.program_id(2)
    lane = lax.broadcasted_iota(jnp.int32, (tq, LANES), 1)
    row = lax.broadcasted_iota(jnp.int32, (tq, tq), 0)
    col = lax.broadcasted_iota(jnp.int32, (tq, tq), 1)
    neg_later = -((row > col).astype(BF16))
    past = col < row
    q2 = q_ref[...] * (HEAD_DIM ** -0.5)
    zero = jnp.zeros_like(q2)
    qm = (jnp.where(lane < HEAD_DIM, q2, zero), jnp.where(lane < HEAD_DIM, zero, q2))

    def block(j, newer, diagonal):
        start = pl.multiple_of(j * tq, tq)
        kb = k_ref[pl.ds(start, tq), :]
        vb = v_ref[pl.ds(start, tq), :]
        out = []
        for half in range(2):
            z2 = lax.dot_general(qm[half], kb, (((1,), (1,)), ((), ())), preferred_element_type=F32) * LOG2E
            sp = jnp.maximum(z2, 0.0) + jnp.log2(1.0 + jnp.exp2(-jnp.abs(z2)))
            if diagonal:
                sp = jnp.where(past, sp, 0.0)
            local = jnp.dot(sp.astype(BF16), neg_later, preferred_element_type=F32)
            a = jnp.exp2((z2 - sp) + (local + newer[half]))
            if diagonal:
                a = jnp.where(past, a, 0.0)
            pv = jnp.dot(a.astype(BF16), vb, preferred_element_type=F32)
            if diagonal:
                acc_ref[half] = pv
            else:
                acc_ref[half] += pv
            out.append(newer[half] - jnp.sum(sp, axis=-1, keepdims=True))
        return tuple(out)

    zeros = jnp.zeros((tq, 1), F32)
    newer = block(qi, (zeros, zeros), True)
    odd = qi % 2
    newer = lax.cond(odd == 1, lambda nw: block(qi - 1, nw, False), lambda nw: nw, newer)

    def two_blocks(p, nw):
        j = qi - 1 - odd - 2 * p
        return block(j - 1, block(j, nw, False), False)

    lax.fori_loop(0, qi // 2, two_blocks, newer)
    o_ref[...] = jnp.where(lane < HEAD_DIM, acc_ref[0], acc_ref[1]).astype(o_ref.dtype)


def _sb_attention(sb, batch, seq, tq=256):
    pairs = SB_WIDTH // LANES
    return pl.pallas_call(
        functools.partial(_sb_kernel, tq=tq),
        grid=(batch, pairs, seq // tq),
        in_specs=[pl.BlockSpec((None, tq, LANES), lambda b, p, i: (b, i, p)),
                  pl.BlockSpec((None, seq, LANES), lambda b, p, i: (b, 0, pairs + p)),
                  pl.BlockSpec((None, seq, LANES), lambda b, p, i: (b, 0, 2 * pairs + p))],
        out_specs=pl.BlockSpec((None, tq, LANES), lambda b, p, i: (b, i, p)),
        out_shape=jax.ShapeDtypeStruct((batch, seq, SB_WIDTH), BF16),
        scratch_shapes=[pltpu.VMEM((2, tq, LANES), F32)],
        compiler_params=_cparams(("parallel", "parallel", "parallel")),
        name="sb_attention",
    )(sb, sb, sb)


def _head_rms(x, g, lane):
    xx = x * x
    lo = lane < HEAD_DIM
    s_lo = jnp.sum(jnp.where(lo, xx, 0.0), axis=-1, keepdims=True)
    s_hi = jnp.sum(jnp.where(lo, 0.0, xx), axis=-1, keepdims=True)
    ms = jnp.where(lo, s_lo, s_hi) * (1.0 / HEAD_DIM)
    return x * lax.rsqrt(ms + EPS) * g


def _dil_kernel(q_ref, k_ref, v_ref, gq_ref, gk_ref, o_ref, l_ref, qn_ref, kn_ref, vn_ref,
                *, dilation, length, slopes, scale, unroll):
    nb = length // DIL_BLOCK
    pair = pl.program_id(1)
    lane_l = lax.broadcasted_iota(jnp.int32, (length, LANES), 1)
    lane_b = lax.broadcasted_iota(jnp.int32, (DIL_BLOCK, LANES), 1)
    qi = lax.broadcasted_iota(jnp.int32, (DIL_BLOCK, 2 * DIL_BLOCK), 0)
    kj = lax.broadcasted_iota(jnp.int32, (DIL_BLOCK, 2 * DIL_BLOCK), 1) - DIL_BLOCK
    gap = qi - kj
    band = (gap >= 0) & (gap <= DIL_BLOCK)
    gapf = (gap * dilation).astype(F32)
    bias_rest, bias_first = [], []
    for half in range(2):
        slope = jnp.where(pair == 0, slopes[half], slopes[2 + half])
        bias_rest.append(jnp.where(band, -(slope * gapf), NEG_BIG))
        bias_first.append(jnp.where(band & (kj >= 0), -(slope * gapf), NEG_BIG))

    zero_block = jnp.zeros((DIL_BLOCK, LANES), BF16)
    for c in range(dilation):
        rows = pl.ds(0, length) if dilation == 1 else pl.ds(c, length, stride=dilation)
        qn_ref[c] = _head_rms(q_ref[rows, :], gq_ref[...], lane_l).astype(BF16) * scale
        kn_ref[c, pl.ds(0, DIL_BLOCK), :] = zero_block
        vn_ref[c, pl.ds(0, DIL_BLOCK), :] = zero_block
        kn_ref[c, pl.ds(DIL_BLOCK, length), :] = _head_rms(k_ref[rows, :], gk_ref[...], lane_l).astype(BF16)
        vn_ref[c, pl.ds(DIL_BLOCK, length), :] = v_ref[rows, :].astype(BF16)

    def block(b):
        c, n = b // nb, b % nb
        start = pl.multiple_of(n * DIL_BLOCK, DIL_BLOCK)
        qb = qn_ref[c, pl.ds(start, DIL_BLOCK), :]
        kw = kn_ref[c, pl.ds(start, 2 * DIL_BLOCK), :]
        vw = vn_ref[c, pl.ds(start, 2 * DIL_BLOCK), :]
        outs, lses = [], []
        for half in range(2):
            in_head = (lane_b < HEAD_DIM) if half == 0 else (lane_b >= HEAD_DIM)
            qm = jnp.where(in_head, qb, jnp.zeros_like(qb))
            s = lax.dot_general(qm, kw, (((1,), (1,)), ((), ())), preferred_element_type=F32)
            s = s + jnp.where(n == 0, bias_first[half], bias_rest[half])
            m = jnp.max(s, axis=-1, keepdims=True)
            e = jnp.exp(s - m)
            denom = jnp.sum(e, axis=-1, keepdims=True)
            o = jnp.dot(e.astype(BF16), vw, preferred_element_type=F32) / denom
            outs.append(o)
            lses.append(m + jnp.log(denom))
        lo = lane_b < HEAD_DIM
        if dilation == 1:
            dst = pl.ds(start, DIL_BLOCK)
        else:
            dst = pl.ds(c + start * dilation, DIL_BLOCK, stride=dilation)
        o_ref[dst, :] = jnp.where(lo, outs[0], outs[1])
        l_ref[dst, :] = jnp.where(lo, lses[0], lses[1])

    def blocks(g, _):
        for k in range(unroll):
            block(g * unroll + k)
        return 0

    lax.fori_loop(0, dilation * nb // unroll, blocks, 0)


def _dil_group(dil, g_q2, g_k2, group, batch, seq):
    _, dilation = DIL_CONFIG[group]
    length = seq // dilation
    blocks = DIL_WIDTH // LANES
    pairs = DIL_OUT_WIDTH // LANES
    all_slopes = [2.0 ** (-ALIBI_MAX_BIAS * (i + 1) / DIL_HEADS) for i in range(DIL_HEADS)]
    slopes = tuple(float(np.float32(s)) for s in
                   all_slopes[group * DIL_HEADS_PER_GROUP:(group + 1) * DIL_HEADS_PER_GROUP])
    kern = functools.partial(_dil_kernel, dilation=dilation, length=length, slopes=slopes,
                             scale=HEAD_DIM ** -0.5, unroll=4)
    col = lambda off: (lambda b, p: (b, 0, off + group * pairs + p))
    out_sd = jax.ShapeDtypeStruct((batch, seq, DIL_OUT_WIDTH), F32)
    return pl.pallas_call(
        kern,
        grid=(batch, pairs),
        in_specs=[pl.BlockSpec((None, seq, LANES), col(0)),
                  pl.BlockSpec((None, seq, LANES), col(blocks)),
                  pl.BlockSpec((None, seq, LANES), col(2 * blocks)),
                  pl.BlockSpec((1, LANES), lambda b, p: (0, 0)),
                  pl.BlockSpec((1, LANES), lambda b, p: (0, 0))],
        out_specs=[pl.BlockSpec((None, seq, LANES), lambda b, p: (b, 0, p)),
                   pl.BlockSpec((None, seq, LANES), lambda b, p: (b, 0, p))],
        out_shape=[out_sd, out_sd],
        scratch_shapes=[pltpu.VMEM((dilation, length, LANES), BF16),
                        pltpu.VMEM((dilation, length + DIL_BLOCK, LANES), BF16),
                        pltpu.VMEM((dilation, length + DIL_BLOCK, LANES), BF16)],
        compiler_params=_cparams(("parallel", "parallel")),
        name=f"dilated_attention_r{dilation}",
    )(dil, dil, dil, g_q2, g_k2)


def _mem_kernel(q_ref, k_ref, v_ref, gq_ref, gk_ref, o_ref, *, scale):
    q = _rms(q_ref[...], gq_ref[...]).astype(BF16)
    k = _rms(k_ref[...], gk_ref[...]).astype(BF16)
    s = lax.dot_general(q, k, (((1,), (1,)), ((), ())), preferred_element_type=F32) * scale
    m = jnp.max(s, axis=-1, keepdims=True)
    e = jnp.exp(s - m)
    denom = jnp.sum(e, axis=-1, keepdims=True)
    o = jnp.dot(e.astype(BF16), v_ref[...].astype(BF16), preferred_element_type=F32) / denom
    o_ref[...] = o.astype(o_ref.dtype)


def _mem_attention(mq, kv, g_q, g_k, batch, seq, mem_len, tq=512):
    return pl.pallas_call(
        functools.partial(_mem_kernel, scale=MEM_HEAD_DIM ** -0.5),
        grid=(batch, MEM_HEADS, seq // tq),
        in_specs=[pl.BlockSpec((None, tq, LANES), lambda b, h, i: (b, i, h)),
                  pl.BlockSpec((None, mem_len, LANES), lambda b, h, i: (b, 0, h)),
                  pl.BlockSpec((None, mem_len, LANES), lambda b, h, i: (b, 0, MEM_HEADS + h)),
                  pl.BlockSpec((1, LANES), lambda b, h, i: (0, 0)),
                  pl.BlockSpec((1, LANES), lambda b, h, i: (0, 0))],
        out_specs=pl.BlockSpec((None, tq, LANES), lambda b, h, i: (b, i, h)),
        out_shape=jax.ShapeDtypeStruct((batch, seq, MEM_WIDTH), BF16),
        compiler_params=_cparams(("parallel", "parallel", "parallel")),
        name="memory_attention",
    )(mq, kv, kv, g_q.reshape(1, LANES), g_k.reshape(1, LANES))


def _merge_kernel(x_ref, g_ref, ysb_ref, o0_ref, o1_ref, o2_ref, l0_ref, l1_ref, l2_ref, ymem_ref,
                  wg_ref, bg_ref, wsb_ref, wdil_ref, wmem_ref, wout_ref, out_ref):
    d = x_ref.shape[-1]
    x = x_ref[...]
    h = _rms(x, g_ref[...]).astype(BF16)
    l0, l1, l2 = l0_ref[...], l1_ref[...], l2_ref[...]
    m = jnp.maximum(jnp.maximum(l0, l1), l2)
    e0, e1, e2 = jnp.exp(l0 - m), jnp.exp(l1 - m), jnp.exp(l2 - m)
    inv = 1.0 / (e0 + e1 + e2)
    y_dil = ((e0 * inv) * o0_ref[...] + (e1 * inv) * o1_ref[...] + (e2 * inv) * o2_ref[...]).astype(BF16)
    merged = jnp.zeros(x.shape, F32)
    branches = ((ysb_ref[...], wsb_ref), (y_dil, wdil_ref), (ymem_ref[...], wmem_ref))
    for n, (y, w_ref) in enumerate(branches):
        pre = jnp.dot(h, wg_ref[:, n * d:(n + 1) * d], preferred_element_type=F32) + bg_ref[:, n * d:(n + 1) * d]
        gate = jax.nn.sigmoid(pre)
        merged = merged + gate * jnp.dot(y, w_ref[...], preferred_element_type=F32)
    out_ref[...] = x + jnp.dot(merged.astype(BF16), wout_ref[...], preferred_element_type=F32)


def _merge(x2d, g_mix, y_sb, o_list, l_list, y_mem, w_gate, b_gate, w_o_sb, w_o_dil, w_o_mem, w_out, tm=256):
    t, d = x2d.shape
    row = lambda w: pl.BlockSpec((tm, w), lambda i: (i, 0))
    full = lambda a: pl.BlockSpec(a.shape, lambda i: (0, 0))
    g2 = g_mix.reshape(1, d)
    b2 = b_gate.reshape(1, -1)
    args = [x2d, g2, y_sb, *o_list, *l_list, y_mem, w_gate, b2, w_o_sb, w_o_dil, w_o_mem, w_out]
    specs = [row(d), full(g2), row(SB_WIDTH)] + [row(DIL_OUT_WIDTH)] * 6 + [row(MEM_WIDTH)] + \
            [full(a) for a in (w_gate, b2, w_o_sb, w_o_dil, w_o_mem, w_out)]
    return pl.pallas_call(
        _merge_kernel,
        grid=(t // tm,),
        in_specs=specs,
        out_specs=row(d),
        out_shape=jax.ShapeDtypeStruct((t, d), F32),
        compiler_params=_cparams(("parallel",)),
        name="merge_project",
    )(*args)


def _oddeven_merge_sort_pairs(n):
    pairs = []

    def merge(lo, hi, r):
        step = r * 2
        if step < hi - lo:
            merge(lo, hi, step)
            merge(lo + r, hi, step)
            for i in range(lo + r, hi - r, step):
                pairs.append((i, i + r))
        else:
            pairs.append((lo, lo + r))

    def sort(lo, hi):
        if hi - lo >= 1:
            mid = lo + (hi - lo) // 2
            sort(lo, mid)
            sort(mid + 1, hi)
            merge(lo, hi, 1)

    sort(0, n - 1)
    return pairs


_SORT16 = _oddeven_merge_sort_pairs(PEER_TOPK)
_CAND = [(a, b) for a in range(PEER_TOPK) for b in range(PEER_TOPK) if (a + 1) * (b + 1) <= PEER_TOPK]


def _top16_sorted(sc):
    n = PEER_KEYS // 8
    v = [sc[8 * i:8 * (i + 1), :] for i in range(n)]
    for (i, j) in _SORT16:
        hi, lo = jnp.maximum(v[i], v[j]), jnp.minimum(v[i], v[j])
        v[i], v[j] = hi, lo
    for shift in (4, 2, 1):
        other = [pltpu.roll(x, shift, 0) for x in v]
        v = [jnp.maximum(v[i], other[n - 1 - i]) for i in range(n)]
        d = n // 2
        while d >= 1:
            for i in range(n):
                if (i // d) % 2 == 0:
                    hi, lo = jnp.maximum(v[i], v[i + d]), jnp.minimum(v[i], v[i + d])
                    v[i], v[i + d] = hi, lo
            d //= 2
    return v


def _peer_select_kernel(x_ref, g_ref, wq_ref, sk_ref, h2_ref, c1_ref, e1_ref, r2_ref, e2_ref,
                        q_ref, sc_ref, top_ref, cand_ref, tau_ref, invz_ref):
    tm = x_ref.shape[0]
    h2 = _rms(x_ref[...], g_ref[...]).astype(BF16)
    h2_ref[...] = h2
    q_ref[...] = jnp.dot(h2, wq_ref[...], preferred_element_type=F32).astype(BF16)
    for hp in range(2 * PEER_HEADS):
        sc_ref[hp] = lax.dot_general(sk_ref[hp], q_ref[:, hp * PEER_HALF:(hp + 1) * PEER_HALF],
                                     (((1,), (1,)), ((), ())), preferred_element_type=F32)

    def sort_head(h, _):
        for p in range(2):
            top = _top16_sorted(sc_ref[2 * h + p])
            for a in range(PEER_TOPK):
                top_ref[p, a, pl.ds(h, 1), :] = top[a][0:1, :]
        return 0

    lax.fori_loop(0, PEER_HEADS, sort_head, 0)

    for n, (a, b) in enumerate(_CAND):
        cand_ref[n] = top_ref[0, a] + top_ref[1, b]

    def kth_largest(n, tau):
        ci = cand_ref[n]
        cnt = jnp.zeros((8, tm), F32)
        for m in range(len(_CAND)):
            cnt = cnt + jnp.where(cand_ref[m] >= ci, 1.0, 0.0)
        return jnp.maximum(tau, jnp.where(cnt >= float(PEER_TOPK), ci, -jnp.inf))

    tau = lax.fori_loop(0, len(_CAND), kth_largest, jnp.full((8, tm), -jnp.inf, F32))
    tau_ref[...] = tau
    top_sum = cand_ref[0]
    z = jnp.zeros((8, tm), F32)
    for n in range(len(_CAND)):
        ci = cand_ref[n]
        z = z + jnp.where(ci >= tau, jnp.exp(ci - top_sum), 0.0)
    invz_ref[...] = 0.5 / z

    def emit(h, _):
        row = pl.ds(h, 1)
        s1, s2 = sc_ref[2 * h], sc_ref[2 * h + 1]
        tau_h = tau_ref[row, :]
        cnt = jnp.zeros(s1.shape, F32)
        rank = jnp.zeros(s2.shape, F32)
        for b in range(PEER_TOPK):
            s2b = top_ref[1, b, row, :]
            cnt = jnp.where((s1 + s2b) >= tau_h, float(b + 1), cnt)
            rank = jnp.where(s2b > s2, float(b + 1), rank)
        in_top = s1 >= top_ref[0, PEER_TOPK - 1, row, :]
        c1_ref[h] = jnp.where(in_top, cnt, 0.0)
        e1_ref[h] = jnp.exp(s1 - top_ref[0, 0, row, :]) * invz_ref[row, :]
        r2_ref[h] = rank.astype(BF16)
        e2_ref[h] = jnp.exp(s2 - top_ref[1, 0, row, :]).astype(BF16)
        return 0

    lax.fori_loop(0, PEER_HEADS, emit, 0)


def _peer_select(x1, g_ffn, w_q, subkeys, tm=512):
    t, d = x1.shape
    halves = 2 * PEER_HEADS
    sk3 = subkeys.reshape(halves, PEER_KEYS, PEER_HALF)
    shape = (PEER_HEADS, PEER_KEYS, t)
    big_spec = pl.BlockSpec((PEER_HEADS, PEER_KEYS, tm), lambda i: (0, 0, i))
    return pl.pallas_call(
        _peer_select_kernel,
        grid=(t // tm,),
        in_specs=[pl.BlockSpec((tm, d), lambda i: (i, 0)),
                  pl.BlockSpec((1, d), lambda i: (0, 0)),
                  pl.BlockSpec(w_q.shape, lambda i: (0, 0)),
                  pl.BlockSpec(sk3.shape, lambda i: (0, 0, 0))],
        out_specs=[pl.BlockSpec((tm, d), lambda i: (i, 0)), big_spec, big_spec, big_spec, big_spec],
        out_shape=[jax.ShapeDtypeStruct((t, d), BF16),
                   jax.ShapeDtypeStruct(shape, F32), jax.ShapeDtypeStruct(shape, F32),
                   jax.ShapeDtypeStruct(shape, BF16), jax.ShapeDtypeStruct(shape, BF16)],
        scratch_shapes=[pltpu.VMEM((tm, halves * PEER_HALF), BF16),
                        pltpu.VMEM((halves, PEER_KEYS, tm), F32),
                        pltpu.VMEM((2, PEER_TOPK, 8, tm), F32),
                        pltpu.VMEM((len(_CAND), 8, tm), F32),
                        pltpu.VMEM((8, tm), F32),
                        pltpu.VMEM((8, tm), F32)],
        compiler_params=_cparams(("parallel",)),
        name="peer_select",
    )(x1, g_ffn.reshape(1, d), w_q, sk3)


PACK = 16


def _peer_dense_kernel(x_ref, h2_ref, u_ref, vt_ref, c1_ref, e1_ref, r2_in_ref, e2_in_ref,
                       out_ref, acc_ref, w_ref, act_ref, r2_ref, e2_ref, *, rows, chunk):
    j = pl.program_id(1)
    tm = h2_ref.shape[0]

    @pl.when(j == 0)
    def _():
        acc_ref[...] = jnp.zeros(acc_ref.shape, F32)
        r2_ref[...] = r2_in_ref[...]
        e2_ref[...] = e2_in_ref[...]

    act_ref[...] = lax.dot_general(u_ref[...], h2_ref[...], (((1,), (1,)), ((), ())),
                                   preferred_element_type=F32)
    sqrt_half = math.sqrt(0.5)
    group = pl.ds(pl.multiple_of(j * rows, rows), rows)
    n_pack = PEER_KEYS // PACK
    for c in range(tm // chunk):
        tok = slice(c * chunk, (c + 1) * chunk)
        for r0 in range(0, rows, 2):
            gate = [[jnp.zeros((PACK, chunk), BF16) for _ in range(n_pack)] for _ in range(2)]
            for h in range(PEER_HEADS):
                c1g = c1_ref[h, group, tok]
                e1g = e1_ref[h, group, tok]
                c1 = [jnp.broadcast_to(c1g[r0 + k:r0 + k + 1, :], (PACK, chunk)).astype(BF16) for k in range(2)]
                e1 = [jnp.broadcast_to(e1g[r0 + k:r0 + k + 1, :], (PACK, chunk)).astype(BF16) for k in range(2)]
                for m in range(n_pack):
                    r2 = r2_ref[h, m * PACK:(m + 1) * PACK, tok]
                    e2 = e2_ref[h, m * PACK:(m + 1) * PACK, tok]
                    for k in range(2):
                        gate[k][m] = gate[k][m] + jnp.where(r2 < c1[k], e2 * e1[k], jnp.zeros_like(e2))
            for k in range(2):
                base = (r0 + k) * PEER_KEYS
                for m in range(n_pack):
                    a = act_ref[base + m * PACK:base + (m + 1) * PACK, tok]
                    gelu2 = a * (1.0 + lax.erf(a * sqrt_half))
                    w_ref[base + m * PACK:base + (m + 1) * PACK, tok] = gate[k][m] * gelu2.astype(BF16)
    acc_ref[...] += jnp.dot(vt_ref[...], w_ref[...], preferred_element_type=F32)

    @pl.when(j == pl.num_programs(1) - 1)
    def _():
        out_ref[...] = x_ref[...] + acc_ref[...].T


def _peer_dense(x1, h2, u, vt, c1, e1, r2, e2, tm=512, rows=8, chunk=128):
    t, d = x1.shape
    n_exp = u.shape[0]
    tn = rows * PEER_KEYS
    big_spec = pl.BlockSpec((PEER_HEADS, PEER_KEYS, tm), lambda i, j: (0, 0, i))
    return pl.pallas_call(
        functools.partial(_peer_dense_kernel, rows=rows, chunk=chunk),
        grid=(t // tm, n_exp // tn),
        in_specs=[pl.BlockSpec((tm, d), lambda i, j: (i, 0)),
                  pl.BlockSpec((tm, d), lambda i, j: (i, 0)),
                  pl.BlockSpec((tn, d), lambda i, j: (j, 0)),
                  pl.BlockSpec((d, tn), lambda i, j: (0, j)),
                  big_spec, big_spec, big_spec, big_spec],
        out_specs=pl.BlockSpec((tm, d), lambda i, j: (i, 0)),
        out_shape=jax.ShapeDtypeStruct((t, d), F32),
        scratch_shapes=[pltpu.VMEM((d, tm), F32), pltpu.VMEM((tn, tm), BF16), pltpu.VMEM((tn, tm), F32),
                        pltpu.VMEM((PEER_HEADS, PEER_KEYS, tm), BF16),
                        pltpu.VMEM((PEER_HEADS, PEER_KEYS, tm), BF16)],
        compiler_params=_cparams(("parallel", "arbitrary")),
        name="peer_dense",
    )(x1, h2, u, vt, c1, e1, r2, e2)


def kernel(x, mem, g_mix, g_mem, w_in, w_mem_kv, g_q_dil, g_k_dil, g_q_mem, g_k_mem, w_o_sb, w_o_dil,
           w_o_mem, w_gate, b_gate, w_out, g_ffn, w_peer_q, peer_subkeys, peer_u, peer_v):
    batch, seq, d = x.shape
    mem_len = mem.shape[1]
    depth = w_in.shape[0]
    xt = x.reshape(batch * seq, d)
    for l in range(depth):
        w_in_l = w_in[l].astype(BF16)
        c_sb, c_dil = 3 * SB_WIDTH, 3 * SB_WIDTH + 3 * DIL_WIDTH
        sb, dil, mq = _in_proj(xt, g_mix[l], w_in_l, (c_sb, c_dil - c_sb, MEM_WIDTH), (BF16, F32, F32))
        kv = _norm_matmul(mem.reshape(batch * mem_len, d), g_mem[l], w_mem_kv[l].astype(BF16), F32, 512, 512)

        y_sb = _sb_attention(sb.reshape(batch, seq, c_sb), batch, seq)

        g_q2 = jnp.tile(g_q_dil[l], 2).reshape(1, LANES)
        g_k2 = jnp.tile(g_k_dil[l], 2).reshape(1, LANES)
        dil3 = dil.reshape(batch, seq, 3 * DIL_WIDTH)
        o_list, l_list = [], []
        for group in range(len(DIL_CONFIG)):
            o, lse = _dil_group(dil3, g_q2, g_k2, group, batch, seq)
            o_list.append(o.reshape(batch * seq, DIL_OUT_WIDTH))
            l_list.append(lse.reshape(batch * seq, DIL_OUT_WIDTH))

        y_mem = _mem_attention(mq.reshape(batch, seq, MEM_WIDTH), kv.reshape(batch, mem_len, 2 * MEM_WIDTH),
                               g_q_mem[l], g_k_mem[l], batch, seq, mem_len)

        x1 = _merge(xt, g_mix[l], y_sb.reshape(batch * seq, SB_WIDTH), o_list, l_list,
                    y_mem.reshape(batch * seq, MEM_WIDTH), w_gate[l].astype(BF16), b_gate[l],
                    w_o_sb[l].astype(BF16), w_o_dil[l].astype(BF16), w_o_mem[l].astype(BF16),
                    w_out[l].astype(BF16))

        h2, c1, e1, r2, e2 = _peer_select(x1, g_ffn[l], w_peer_q[l].astype(BF16),
                                          peer_subkeys[l].astype(BF16))
        xt = _peer_dense(x1, h2, peer_u[l].astype(BF16), peer_v[l].astype(BF16).T, c1, e1, r2, e2)
    return xt.reshape(batch, seq, d)
```

```python
import functools
import math

import numpy as np
import jax
import jax.numpy as jnp
from jax import lax
from jax.experimental import pallas as pl
from jax.experimental.pallas import tpu as pltpu

F32 = jnp.float32
BF16 = jnp.bfloat16

EPS = 1e-6
HEAD_DIM = 64
LANES = 128
SB_HEADS = 8
SB_WIDTH = SB_HEADS * HEAD_DIM
DIL_CONFIG = ((128, 1), (512, 4), (2048, 16))
DIL_HEADS_PER_GROUP = 4
DIL_HEADS = len(DIL_CONFIG) * DIL_HEADS_PER_GROUP
DIL_WIDTH = DIL_HEADS * HEAD_DIM
DIL_OUT_WIDTH = DIL_HEADS_PER_GROUP * HEAD_DIM
DIL_BLOCK = 128
ALIBI_MAX_BIAS = 8.0
MEM_HEADS = 4
MEM_HEAD_DIM = 128
MEM_WIDTH = MEM_HEADS * MEM_HEAD_DIM
PEER_HEADS = 8
PEER_KEYS = 128
PEER_TOPK = 16
PEER_HALF = 128
NEG_BIG = -1e30
LOG2E = 1.4426950408889634
VMEM_LIMIT = 56 * 1024 * 1024


def _cparams(sem, flags=None):
    return pltpu.CompilerParams(dimension_semantics=sem, vmem_limit_bytes=VMEM_LIMIT, flags=flags)


def _rms(x, g):
    ms = jnp.mean(x * x, axis=-1, keepdims=True)
    return x * lax.rsqrt(ms + EPS) * g


def _norm_matmul_kernel(x_ref, g_ref, w_ref, o_ref, h_ref):
    @pl.when(pl.program_id(1) == 0)
    def _():
        h_ref[...] = _rms(x_ref[...], g_ref[...]).astype(BF16)

    o_ref[...] = jnp.dot(h_ref[...], w_ref[...], preferred_element_type=F32).astype(o_ref.dtype)


def _norm_matmul(x2d, g, w, out_dtype, tm, tn):
    m, k = x2d.shape
    n = w.shape[1]
    return pl.pallas_call(
        _norm_matmul_kernel,
        grid=(m // tm, n // tn),
        in_specs=[pl.BlockSpec((tm, k), lambda i, j: (i, 0)),
                  pl.BlockSpec((1, k), lambda i, j: (0, 0)),
                  pl.BlockSpec((k, tn), lambda i, j: (0, j))],
        out_specs=pl.BlockSpec((tm, tn), lambda i, j: (i, j)),
        out_shape=jax.ShapeDtypeStruct((m, n), out_dtype),
        scratch_shapes=[pltpu.VMEM((tm, k), BF16)],
        compiler_params=_cparams(("parallel", "arbitrary")),
        name="norm_matmul",
    )(x2d, g.reshape(1, k), w)


def _in_proj_kernel(x_ref, g_ref, w_ref, *out_refs, tn):
    h = _rms(x_ref[...], g_ref[...]).astype(BF16)
    col = 0
    for o_ref in out_refs:
        width = o_ref.shape[1]
        for c0 in range(0, width, tn):
            c1 = min(c0 + tn, width)
            o_ref[:, c0:c1] = jnp.dot(h, w_ref[:, col + c0:col + c1],
                                      preferred_element_type=F32).astype(o_ref.dtype)
        col += width


def _in_proj(x2d, g, w, widths, dtypes, tm=512, tn=512):
    m, k = x2d.shape
    return pl.pallas_call(
        functools.partial(_in_proj_kernel, tn=tn),
        grid=(m // tm,),
        in_specs=[pl.BlockSpec((tm, k), lambda i: (i, 0)),
                  pl.BlockSpec((1, k), lambda i: (0, 0)),
                  pl.BlockSpec(w.shape, lambda i: (0, 0))],
        out_specs=[pl.BlockSpec((tm, n), lambda i: (i, 0)) for n in widths],
        out_shape=[jax.ShapeDtypeStruct((m, n), dt) for n, dt in zip(widths, dtypes)],
        compiler_params=_cparams(("parallel",)),
        name="in_proj",
    )(x2d, g.reshape(1, k), w)


def _sb_kernel(q_ref, k_ref, v_ref, o_ref, acc_ref, *, tq):
    qi = pl.program_id(2)
    lane = lax.broadcasted_iota(jnp.int32, (tq, LANES), 1)
    row = lax.broadcasted_iota(jnp.int32, (tq, tq), 0)
    col = lax.broadcasted_iota(jnp.int32, (tq, tq), 1)
    neg_later = -((row > col).astype(BF16))
    past = col < row
    q2 = q_ref[...] * (HEAD_DIM ** -0.5)
    zero = jnp.zeros_like(q2)
    qm = (jnp.where(lane < HEAD_DIM, q2, zero), jnp.where(lane < HEAD_DIM, zero, q2))

    def block(j, newer, diagonal):
        start = pl.multiple_of(j * tq, tq)
        kb = k_ref[pl.ds(start, tq), :]
        vb = v_ref[pl.ds(start, tq), :]
        out = []
        for half in range(2):
            z2 = lax.dot_general(qm[half], kb, (((1,), (1,)), ((), ())), preferred_element_type=F32) * LOG2E
            sp = jnp.maximum(z2, 0.0) + jnp.log2(1.0 + jnp.exp2(-jnp.abs(z2)))
            if diagonal:
                sp = jnp.where(past, sp, 0.0)
            local = jnp.dot(sp.astype(BF16), neg_later, preferred_element_type=F32)
            a = jnp.exp2((z2 - sp) + (local + newer[half]))
            if diagonal:
                a = jnp.where(past, a, 0.0)
            pv = jnp.dot(a.astype(BF16), vb, preferred_element_type=F32)
            if diagonal:
                acc_ref[half] = pv
            else:
                acc_ref[half] += pv
            out.append(newer[half] - jnp.sum(sp, axis=-1, keepdims=True))
        return tuple(out)

    zeros = jnp.zeros((tq, 1), F32)
    newer = block(qi, (zeros, zeros), True)
    odd = qi % 2
    newer = lax.cond(odd == 1, lambda nw: block(qi - 1, nw, False), lambda nw: nw, newer)

    def two_blocks(p, nw):
        j = qi - 1 - odd - 2 * p
        return block(j - 1, block(j, nw, False), False)

    lax.fori_loop(0, qi // 2, two_blocks, newer)
    o_ref[...] = jnp.where(lane < HEAD_DIM, acc_ref[0], acc_ref[1]).astype(o_ref.dtype)


def _sb_attention(sb, batch, seq, tq=256):
    pairs = SB_WIDTH // LANES
    return pl.pallas_call(
        functools.partial(_sb_kernel, tq=tq),
        grid=(batch, pairs, seq // tq),
        in_specs=[pl.BlockSpec((None, tq, LANES), lambda b, p, i: (b, i, p)),
                  pl.BlockSpec((None, seq, LANES), lambda b, p, i: (b, 0, pairs + p)),
                  pl.BlockSpec((None, seq, LANES), lambda b, p, i: (b, 0, 2 * pairs + p))],
        out_specs=pl.BlockSpec((None, tq, LANES), lambda b, p, i: (b, i, p)),
        out_shape=jax.ShapeDtypeStruct((batch, seq, SB_WIDTH), BF16),
        scratch_shapes=[pltpu.VMEM((2, tq, LANES), F32)],
        compiler_params=_cparams(("parallel", "parallel", "parallel")),
        name="sb_attention",
    )(sb, sb, sb)


def _head_rms(x, g, lane):
    xx = x * x
    lo = lane < HEAD_DIM
    s_lo = jnp.sum(jnp.where(lo, xx, 0.0), axis=-1, keepdims=True)
    s_hi = jnp.sum(jnp.where(lo, 0.0, xx), axis=-1, keepdims=True)
    ms = jnp.where(lo, s_lo, s_hi) * (1.0 / HEAD_DIM)
    return x * lax.rsqrt(ms + EPS) * g


def _dil_kernel(q_ref, k_ref, v_ref, gq_ref, gk_ref, o_ref, l_ref, qn_ref, kn_ref, vn_ref,
                *, dilation, length, slopes, scale, unroll):
    nb = length // DIL_BLOCK
    pair = pl.program_id(1)
    lane_l = lax.broadcasted_iota(jnp.int32, (length, LANES), 1)
    lane_b = lax.broadcasted_iota(jnp.int32, (DIL_BLOCK, LANES), 1)
    qi = lax.broadcasted_iota(jnp.int32, (DIL_BLOCK, 2 * DIL_BLOCK), 0)
    kj = lax.broadcasted_iota(jnp.int32, (DIL_BLOCK, 2 * DIL_BLOCK), 1) - DIL_BLOCK
    gap = qi - kj
    band = (gap >= 0) & (gap <= DIL_BLOCK)
    gapf = (gap * dilation).astype(F32)
    bias_rest, bias_first = [], []
    for half in range(2):
        slope = jnp.where(pair == 0, slopes[half], slopes[2 + half])
        bias_rest.append(jnp.where(band, -(slope * gapf), NEG_BIG))
        bias_first.append(jnp.where(band & (kj >= 0), -(slope * gapf), NEG_BIG))

    zero_block = jnp.zeros((DIL_BLOCK, LANES), BF16)
    for c in range(dilation):
        rows = pl.ds(0, length) if dilation == 1 else pl.ds(c, length, stride=dilation)
        qn_ref[c] = _head_rms(q_ref[rows, :], gq_ref[...], lane_l).astype(BF16) * scale
        kn_ref[c, pl.ds(0, DIL_BLOCK), :] = zero_block
        vn_ref[c, pl.ds(0, DIL_BLOCK), :] = zero_block
        kn_ref[c, pl.ds(DIL_BLOCK, length), :] = _head_rms(k_ref[rows, :], gk_ref[...], lane_l).astype(BF16)
        vn_ref[c, pl.ds(DIL_BLOCK, length), :] = v_ref[rows, :].astype(BF16)

    def block(b):
        c, n = b // nb, b % nb
        start = pl.multiple_of(n * DIL_BLOCK, DIL_BLOCK)
        qb = qn_ref[c, pl.ds(start, DIL_BLOCK), :]
        kw = kn_ref[c, pl.ds(start, 2 * DIL_BLOCK), :]
        vw = vn_ref[c, pl.ds(start, 2 * DIL_BLOCK), :]
        outs, lses = [], []
        for half in range(2):
            in_head = (lane_b < HEAD_DIM) if half == 0 else (lane_b >= HEAD_DIM)
            qm = jnp.where(in_head, qb, jnp.zeros_like(qb))
            s = lax.dot_general(qm, kw, (((1,), (1,)), ((), ())), preferred_element_type=F32)
            s = s + jnp.where(n == 0, bias_first[half], bias_rest[half])
            m = jnp.max(s, axis=-1, keepdims=True)
            e = jnp.exp(s - m)
            denom = jnp.sum(e, axis=-1, keepdims=True)
            o = jnp.dot(e.astype(BF16), vw, preferred_element_type=F32) / denom
            outs.append(o)
            lses.append(m + jnp.log(denom))
        lo = lane_b < HEAD_DIM
        if dilation == 1:
            dst = pl.ds(start, DIL_BLOCK)
        else:
            dst = pl.ds(c + start * dilation, DIL_BLOCK, stride=dilation)
        o_ref[dst, :] = jnp.where(lo, outs[0], outs[1])
        l_ref[dst, :] = jnp.where(lo, lses[0], lses[1])

    def blocks(g, _):
        for k in range(unroll):
            block(g * unroll + k)
        return 0

    lax.fori_loop(0, dilation * nb // unroll, blocks, 0)


def _dil_group(dil, g_q2, g_k2, group, batch, seq):
    _, dilation = DIL_CONFIG[group]
    length = seq // dilation
    blocks = DIL_WIDTH // LANES
    pairs = DIL_OUT_WIDTH // LANES
    all_slopes = [2.0 ** (-ALIBI_MAX_BIAS * (i + 1) / DIL_HEADS) for i in range(DIL_HEADS)]
    slopes = tuple(float(np.float32(s)) for s in
                   all_slopes[group * DIL_HEADS_PER_GROUP:(group + 1) * DIL_HEADS_PER_GROUP])
    kern = functools.partial(_dil_kernel, dilation=dilation, length=length, slopes=slopes,
                             scale=HEAD_DIM ** -0.5, unroll=4)
    col = lambda off: (lambda b, p: (b, 0, off + group * pairs + p))
    out_sd = jax.ShapeDtypeStruct((batch, seq, DIL_OUT_WIDTH), F32)
    return pl.pallas_call(
        kern,
        grid=(batch, pairs),
        in_specs=[pl.BlockSpec((None, seq, LANES), col(0)),
                  pl.BlockSpec((None, seq, LANES), col(blocks)),
                  pl.BlockSpec((None, seq, LANES), col(2 * blocks)),
                  pl.BlockSpec((1, LANES), lambda b, p: (0, 0)),
                  pl.BlockSpec((1, LANES), lambda b, p: (0, 0))],
        out_specs=[pl.BlockSpec((None, seq, LANES), lambda b, p: (b, 0, p)),
                   pl.BlockSpec((None, seq, LANES), lambda b, p: (b, 0, p))],
        out_shape=[out_sd, out_sd],
        scratch_shapes=[pltpu.VMEM((dilation, length, LANES), BF16),
                        pltpu.VMEM((dilation, length + DIL_BLOCK, LANES), BF16),
                        pltpu.VMEM((dilation, length + DIL_BLOCK, LANES), BF16)],
        compiler_params=_cparams(("parallel", "parallel")),
        name=f"dilated_attention_r{dilation}",
    )(dil, dil, dil, g_q2, g_k2)


def _mem_kernel(q_ref, k_ref, v_ref, gq_ref, gk_ref, o_ref, *, scale):
    q = _rms(q_ref[...], gq_ref[...]).astype(BF16)
    k = _rms(k_ref[...], gk_ref[...]).astype(BF16)
    s = lax.dot_general(q, k, (((1,), (1,)), ((), ())), preferred_element_type=F32) * scale
    m = jnp.max(s, axis=-1, keepdims=True)
    e = jnp.exp(s - m)
    denom = jnp.sum(e, axis=-1, keepdims=True)
    o = jnp.dot(e.astype(BF16), v_ref[...].astype(BF16), preferred_element_type=F32) / denom
    o_ref[...] = o.astype(o_ref.dtype)


def _mem_attention(mq, kv, g_q, g_k, batch, seq, mem_len, tq=512):
    return pl.pallas_call(
        functools.partial(_mem_kernel, scale=MEM_HEAD_DIM ** -0.5),
        grid=(batch, MEM_HEADS, seq // tq),
        in_specs=[pl.BlockSpec((None, tq, LANES), lambda b, h, i: (b, i, h)),
                  pl.BlockSpec((None, mem_len, LANES), lambda b, h, i: (b, 0, h)),
                  pl.BlockSpec((None, mem_len, LANES), lambda b, h, i: (b, 0, MEM_HEADS + h)),
                  pl.BlockSpec((1, LANES), lambda b, h, i: (0, 0)),
                  pl.BlockSpec((1, LANES), lambda b, h, i: (0, 0))],
        out_specs=pl.BlockSpec((None, tq, LANES), lambda b, h, i: (b, i, h)),
        out_shape=jax.ShapeDtypeStruct((batch, seq, MEM_WIDTH), BF16),
        compiler_params=_cparams(("parallel", "parallel", "parallel")),
        name="memory_attention",
    )(mq, kv, kv, g_q.reshape(1, LANES), g_k.reshape(1, LANES))


def _merge_kernel(x_ref, g_ref, ysb_ref, o0_ref, o1_ref, o2_ref, l0_ref, l1_ref, l2_ref, ymem_ref,
                  wg_ref, bg_ref, wsb_ref, wdil_ref, wmem_ref, wout_ref, out_ref):
    d = x_ref.shape[-1]
    x = x_ref[...]
    h = _rms(x, g_ref[...]).astype(BF16)
    l0, l1, l2 = l0_ref[...], l1_ref[...], l2_ref[...]
    m = jnp.maximum(jnp.maximum(l0, l1), l2)
    e0, e1, e2 = jnp.exp(l0 - m), jnp.exp(l1 - m), jnp.exp(l2 - m)
    inv = 1.0 / (e0 + e1 + e2)
    y_dil = ((e0 * inv) * o0_ref[...] + (e1 * inv) * o1_ref[...] + (e2 * inv) * o2_ref[...]).astype(BF16)
    merged = jnp.zeros(x.shape, F32)
    branches = ((ysb_ref[...], wsb_ref), (y_dil, wdil_ref), (ymem_ref[...], wmem_ref))
    for n, (y, w_ref) in enumerate(branches):
        pre = jnp.dot(h, wg_ref[:, n * d:(n + 1) * d], preferred_element_type=F32) + bg_ref[:, n * d:(n + 1) * d]
        gate = jax.nn.sigmoid(pre)
        merged = merged + gate * jnp.dot(y, w_ref[...], preferred_element_type=F32)
    out_ref[...] = x + jnp.dot(merged.astype(BF16), wout_ref[...], preferred_element_type=F32)


def _merge(x2d, g_mix, y_sb, o_list, l_list, y_mem, w_gate, b_gate, w_o_sb, w_o_dil, w_o_mem, w_out, tm=256):
    t, d = x2d.shape
    row = lambda w: pl.BlockSpec((tm, w), lambda i: (i, 0))
    full = lambda a: pl.BlockSpec(a.shape, lambda i: (0, 0))
    g2 = g_mix.reshape(1, d)
    b2 = b_gate.reshape(1, -1)
    args = [x2d, g2, y_sb, *o_list, *l_list, y_mem, w_gate, b2, w_o_sb, w_o_dil, w_o_mem, w_out]
    specs = [row(d), full(g2), row(SB_WIDTH)] + [row(DIL_OUT_WIDTH)] * 6 + [row(MEM_WIDTH)] + \
            [full(a) for a in (w_gate, b2, w_o_sb, w_o_dil, w_o_mem, w_out)]
    return pl.pallas_call(
        _merge_kernel,
        grid=(t // tm,),
        in_specs=specs,
        out_specs=row(d),
        out_shape=jax.ShapeDtypeStruct((t, d), F32),
        compiler_params=_cparams(("parallel",)),
        name="merge_project",
    )(*args)


def _oddeven_merge_sort_pairs(n):
    pairs = []

    def merge(lo, hi, r):
        step = r * 2
        if step < hi - lo:
            merge(lo, hi, step)
            merge(lo + r, hi, step)
            for i in range(lo + r, hi - r, step):
                pairs.append((i, i + r))
        else:
            pairs.append((lo, lo + r))

    def sort(lo, hi):
        if hi - lo >= 1:
            mid = lo + (hi - lo) // 2
            sort(lo, mid)
            sort(mid + 1, hi)
            merge(lo, hi, 1)

    sort(0, n - 1)
    return pairs


_SORT16 = _oddeven_merge_sort_pairs(PEER_TOPK)
_CAND = [(a, b) for a in range(PEER_TOPK) for b in range(PEER_TOPK) if (a + 1) * (b + 1) <= PEER_TOPK]


def _top16_sorted(sc):
    n = PEER_KEYS // 8
    v = [sc[8 * i:8 * (i + 1), :] for i in range(n)]
    for (i, j) in _SORT16:
        hi, lo = jnp.maximum(v[i], v[j]), jnp.minimum(v[i], v[j])
        v[i], v[j] = hi, lo
    for shift in (4, 2, 1):
        other = [pltpu.roll(x, shift, 0) for x in v]
        v = [jnp.maximum(v[i], other[n - 1 - i]) for i in range(n)]
        d = n // 2
        while d >= 1:
            for i in range(n):
                if (i // d) % 2 == 0:
                    hi, lo = jnp.maximum(v[i], v[i + d]), jnp.minimum(v[i], v[i + d])
                    v[i], v[i + d] = hi, lo
            d //= 2
    return v


def _exchange(v, i, j):
    a, b = v[i], v[j]
    if b is None:
        return
    if a is None:
        v[i], v[j] = b, None
    else:
        v[i], v[j] = jnp.maximum(a, b), jnp.minimum(a, b)


def _bitonic_descending(v):
    n = len(v)
    d = n // 2
    while d >= 1:
        for i in range(n):
            if (i // d) % 2 == 0:
                _exchange(v, i, i + d)
        d //= 2


def _largest16(vals):
    n = PEER_TOPK
    groups = []
    for g0 in range(0, len(vals), n):
        g = list(vals[g0:g0 + n]) + [None] * max(0, g0 + n - len(vals))
        for (i, j) in _SORT16:
            _exchange(g, i, j)
        groups.append(g)
    while len(groups) > 1:
        merged = []
        for k in range(0, len(groups) - 1, 2):
            a, b = groups[k], groups[k + 1]
            m = []
            for i in range(n):
                x, y = a[i], b[n - 1 - i]
                m.append(x if y is None else y if x is None else jnp.maximum(x, y))
            _bitonic_descending(m)
            merged.append(m)
        if len(groups) % 2:
            merged.append(groups[-1])
        groups = merged
    return groups[0]


def _peer_select_kernel(x_ref, g_ref, wq_ref, sk_ref, h2_ref, c1_ref, e1_ref, r2_ref, e2_ref,
                        q_ref, sc_ref, top_ref, cand_ref, tau_ref, invz_ref):
    tm = x_ref.shape[0]
    h2 = _rms(x_ref[...], g_ref[...]).astype(BF16)
    h2_ref[...] = h2
    q_ref[...] = jnp.dot(h2, wq_ref[...], preferred_element_type=F32).astype(BF16)
    for hp in range(2 * PEER_HEADS):
        sc_ref[hp] = lax.dot_general(sk_ref[hp], q_ref[:, hp * PEER_HALF:(hp + 1) * PEER_HALF],
                                     (((1,), (1,)), ((), ())), preferred_element_type=F32)

    def sort_head(h, _):
        for p in range(2):
            top = _top16_sorted(sc_ref[2 * h + p])
            for a in range(PEER_TOPK):
                top_ref[p, a, pl.ds(h, 1), :] = top[a][0:1, :]
        return 0

    lax.fori_loop(0, PEER_HEADS, sort_head, 0)

    for n, (a, b) in enumerate(_CAND):
        cand_ref[n] = top_ref[0, a] + top_ref[1, b]

    lanes = 2 * LANES
    for c0 in range(0, tm, lanes):
        top = _largest16([cand_ref[n, :, c0:c0 + lanes] for n in range(len(_CAND))])
        tau_ref[:, c0:c0 + lanes] = top[PEER_TOPK - 1]
    tau = tau_ref[...]
    top_sum = cand_ref[0]
    z = jnp.zeros((8, tm), F32)
    for n in range(len(_CAND)):
        ci = cand_ref[n]
        z = z + jnp.where(ci >= tau, jnp.exp(ci - top_sum), 0.0)
    invz_ref[...] = 0.5 / z

    def emit(h, _):
        row = pl.ds(h, 1)
        s1, s2 = sc_ref[2 * h], sc_ref[2 * h + 1]
        tau_h = tau_ref[row, :]
        cnt = jnp.zeros(s1.shape, F32)
        rank = jnp.zeros(s2.shape, F32)
        for b in range(PEER_TOPK):
            s2b = top_ref[1, b, row, :]
            cnt = jnp.where((s1 + s2b) >= tau_h, float(b + 1), cnt)
            rank = jnp.where(s2b > s2, float(b + 1), rank)
        in_top = s1 >= top_ref[0, PEER_TOPK - 1, row, :]
        c1_ref[h] = jnp.where(in_top, cnt, 0.0)
        e1_ref[h] = jnp.exp(s1 - top_ref[0, 0, row, :]) * invz_ref[row, :]
        r2_ref[h] = rank.astype(BF16)
        e2_ref[h] = jnp.exp(s2 - top_ref[1, 0, row, :]).astype(BF16)
        return 0

    lax.fori_loop(0, PEER_HEADS, emit, 0)


def _peer_select(x1, g_ffn, w_q, subkeys, tm=512):
    t, d = x1.shape
    halves = 2 * PEER_HEADS
    sk3 = subkeys.reshape(halves, PEER_KEYS, PEER_HALF)
    shape = (PEER_HEADS, PEER_KEYS, t)
    big_spec = pl.BlockSpec((PEER_HEADS, PEER_KEYS, tm), lambda i: (0, 0, i))
    return pl.pallas_call(
        _peer_select_kernel,
        grid=(t // tm,),
        in_specs=[pl.BlockSpec((tm, d), lambda i: (i, 0)),
                  pl.BlockSpec((1, d), lambda i: (0, 0)),
                  pl.BlockSpec(w_q.shape, lambda i: (0, 0)),
                  pl.BlockSpec(sk3.shape, lambda i: (0, 0, 0))],
        out_specs=[pl.BlockSpec((tm, d), lambda i: (i, 0)), big_spec, big_spec, big_spec, big_spec],
        out_shape=[jax.ShapeDtypeStruct((t, d), BF16),
                   jax.ShapeDtypeStruct(shape, F32), jax.ShapeDtypeStruct(shape, F32),
                   jax.ShapeDtypeStruct(shape, BF16), jax.ShapeDtypeStruct(shape, BF16)],
        scratch_shapes=[pltpu.VMEM((tm, halves * PEER_HALF), BF16),
                        pltpu.VMEM((halves, PEER_KEYS, tm), F32),
                        pltpu.VMEM((2, PEER_TOPK, 8, tm), F32),
                        pltpu.VMEM((len(_CAND), 8, tm), F32),
                        pltpu.VMEM((8, tm), F32),
                        pltpu.VMEM((8, tm), F32)],
        compiler_params=_cparams(("parallel",)),
        name="peer_select",
    )(x1, g_ffn.reshape(1, d), w_q, sk3)


PACK = 16
DENSE_ROWS = 8


def _peer_dense_kernel(x_ref, h2_ref, u_ref, vt_ref, c1_ref, e1_ref, r2_in_ref, e2_in_ref,
                       out_ref, acc_ref, w_ref, act_ref, r2_ref, e2_ref, *, rows, chunk):
    j = pl.program_id(1)
    tm = h2_ref.shape[0]

    @pl.when(j == 0)
    def _():
        acc_ref[...] = jnp.zeros(acc_ref.shape, F32)
        r2_ref[...] = r2_in_ref[...]
        e2_ref[...] = e2_in_ref[...]

    act_ref[...] = lax.dot_general(u_ref[...], h2_ref[...], (((1,), (1,)), ((), ())),
                                   preferred_element_type=F32)
    sqrt_half = math.sqrt(0.5)
    n_pack = PEER_KEYS // PACK
    for c in range(tm // chunk):
        tok = slice(c * chunk, (c + 1) * chunk)
        for r0 in range(0, rows, 2):
            gate = [[jnp.zeros((PACK, chunk), BF16) for _ in range(n_pack)] for _ in range(2)]
            for h in range(PEER_HEADS):
                c1g = c1_ref[h, :, tok]
                e1g = e1_ref[h, :, tok]
                c1 = [jnp.broadcast_to(c1g[r0 + k:r0 + k + 1, :], (PACK, chunk)).astype(BF16) for k in range(2)]
                e1 = [jnp.broadcast_to(e1g[r0 + k:r0 + k + 1, :], (PACK, chunk)).astype(BF16) for k in range(2)]
                for m in range(n_pack):
                    r2 = r2_ref[h, m * PACK:(m + 1) * PACK, tok]
                    e2 = e2_ref[h, m * PACK:(m + 1) * PACK, tok]
                    for k in range(2):
                        gate[k][m] = gate[k][m] + jnp.where(r2 < c1[k], e2 * e1[k], jnp.zeros_like(e2))
            for k in range(2):
                base = (r0 + k) * PEER_KEYS
                for m in range(n_pack):
                    a = act_ref[base + m * PACK:base + (m + 1) * PACK, tok]
                    gelu2 = a * (1.0 + lax.erf(a * sqrt_half))
                    w_ref[base + m * PACK:base + (m + 1) * PACK, tok] = gate[k][m] * gelu2.astype(BF16)
    acc_ref[...] += jnp.dot(vt_ref[...], w_ref[...], preferred_element_type=F32)

    @pl.when(j == pl.num_programs(1) - 1)
    def _():
        out_ref[...] = x_ref[...] + acc_ref[...].T


def _peer_dense(x1, h2, u, vt, c1, e1, r2, e2, tm=1024, rows=DENSE_ROWS, chunk=128):
    t, d = x1.shape
    n_exp = u.shape[0]
    tn = rows * PEER_KEYS
    big_spec = pl.BlockSpec((PEER_HEADS, PEER_KEYS, tm), lambda i, j: (0, 0, i))
    row_spec = pl.BlockSpec((PEER_HEADS, rows, tm), lambda i, j: (0, j, i))
    return pl.pallas_call(
        functools.partial(_peer_dense_kernel, rows=rows, chunk=chunk),
        grid=(t // tm, n_exp // tn),
        in_specs=[pl.BlockSpec((tm, d), lambda i, j: (i, 0)),
                  pl.BlockSpec((tm, d), lambda i, j: (i, 0)),
                  pl.BlockSpec((tn, d), lambda i, j: (j, 0)),
                  pl.BlockSpec((None, d, tn), lambda i, j: (j, 0, 0)),
                  row_spec, row_spec, big_spec, big_spec],
        out_specs=pl.BlockSpec((tm, d), lambda i, j: (i, 0)),
        out_shape=jax.ShapeDtypeStruct((t, d), F32),
        scratch_shapes=[pltpu.VMEM((d, tm), F32), pltpu.VMEM((tn, tm), BF16), pltpu.VMEM((tn, tm), F32),
                        pltpu.VMEM((PEER_HEADS, PEER_KEYS, tm), BF16),
                        pltpu.VMEM((PEER_HEADS, PEER_KEYS, tm), BF16)],
        compiler_params=_cparams(("parallel", "arbitrary")),
        name="peer_dense",
    )(x1, h2, u, vt, c1, e1, r2, e2)


def kernel(x, mem, g_mix, g_mem, w_in, w_mem_kv, g_q_dil, g_k_dil, g_q_mem, g_k_mem, w_o_sb, w_o_dil,
           w_o_mem, w_gate, b_gate, w_out, g_ffn, w_peer_q, peer_subkeys, peer_u, peer_v):
    batch, seq, d = x.shape
    mem_len = mem.shape[1]
    depth = w_in.shape[0]
    xt = x.reshape(batch * seq, d)
    for l in range(depth):
        w_in_l = w_in[l].astype(BF16)
        c_sb, c_dil = 3 * SB_WIDTH, 3 * SB_WIDTH + 3 * DIL_WIDTH
        sb, dil, mq = _in_proj(xt, g_mix[l], w_in_l, (c_sb, c_dil - c_sb, MEM_WIDTH), (BF16, F32, F32))
        kv = _norm_matmul(mem.reshape(batch * mem_len, d), g_mem[l], w_mem_kv[l].astype(BF16), F32, 512, 512)

        y_sb = _sb_attention(sb.reshape(batch, seq, c_sb), batch, seq)

        g_q2 = jnp.tile(g_q_dil[l], 2).reshape(1, LANES)
        g_k2 = jnp.tile(g_k_dil[l], 2).reshape(1, LANES)
        dil3 = dil.reshape(batch, seq, 3 * DIL_WIDTH)
        o_list, l_list = [], []
        for group in range(len(DIL_CONFIG)):
            o, lse = _dil_group(dil3, g_q2, g_k2, group, batch, seq)
            o_list.append(o.reshape(batch * seq, DIL_OUT_WIDTH))
            l_list.append(lse.reshape(batch * seq, DIL_OUT_WIDTH))

        y_mem = _mem_attention(mq.reshape(batch, seq, MEM_WIDTH), kv.reshape(batch, mem_len, 2 * MEM_WIDTH),
                               g_q_mem[l], g_k_mem[l], batch, seq, mem_len)

        x1 = _merge(xt, g_mix[l], y_sb.reshape(batch * seq, SB_WIDTH), o_list, l_list,
                    y_mem.reshape(batch * seq, MEM_WIDTH), w_gate[l].astype(BF16), b_gate[l],
                    w_o_sb[l].astype(BF16), w_o_dil[l].astype(BF16), w_o_mem[l].astype(BF16),
                    w_out[l].astype(BF16))

        h2, c1, e1, r2, e2 = _peer_select(x1, g_ffn[l], w_peer_q[l].astype(BF16),
                                          peer_subkeys[l].astype(BF16))
        n_exp = peer_v.shape[1]
        tn = DENSE_ROWS * PEER_KEYS
        vt = peer_v[l].astype(BF16).reshape(n_exp // tn, tn, d).transpose(0, 2, 1)
        xt = _peer_dense(x1, h2, peer_u[l].astype(BF16), vt, c1, e1, r2, e2)
    return xt.reshape(batch, seq, d)
```

```python
import functools
import math

import numpy as np
import jax
import jax.numpy as jnp
from jax import lax
from jax.experimental import pallas as pl
from jax.experimental.pallas import tpu as pltpu

F32 = jnp.float32
BF16 = jnp.bfloat16

EPS = 1e-6
HEAD_DIM = 64
LANES = 128
SB_HEADS = 8
SB_WIDTH = SB_HEADS * HEAD_DIM
DIL_CONFIG = ((128, 1), (512, 4), (2048, 16))
DIL_HEADS_PER_GROUP = 4
DIL_HEADS = len(DIL_CONFIG) * DIL_HEADS_PER_GROUP
DIL_WIDTH = DIL_HEADS * HEAD_DIM
DIL_OUT_WIDTH = DIL_HEADS_PER_GROUP * HEAD_DIM
DIL_BLOCK = 128
ALIBI_MAX_BIAS = 8.0
MEM_HEADS = 4
MEM_HEAD_DIM = 128
MEM_WIDTH = MEM_HEADS * MEM_HEAD_DIM
PEER_HEADS = 8
PEER_KEYS = 128
PEER_TOPK = 16
PEER_HALF = 128
NEG_BIG = -1e30
LOG2E = 1.4426950408889634
VMEM_LIMIT = 56 * 1024 * 1024


def _cparams(sem, flags=None):
    return pltpu.CompilerParams(dimension_semantics=sem, vmem_limit_bytes=VMEM_LIMIT, flags=flags)


def _rms(x, g):
    ms = jnp.mean(x * x, axis=-1, keepdims=True)
    return x * lax.rsqrt(ms + EPS) * g


def _norm_matmul_kernel(x_ref, g_ref, w_ref, o_ref, h_ref):
    @pl.when(pl.program_id(1) == 0)
    def _():
        h_ref[...] = _rms(x_ref[...], g_ref[...]).astype(BF16)

    o_ref[...] = jnp.dot(h_ref[...], w_ref[...], preferred_element_type=F32).astype(o_ref.dtype)


def _norm_matmul(x2d, g, w, out_dtype, tm, tn):
    m, k = x2d.shape
    n = w.shape[1]
    return pl.pallas_call(
        _norm_matmul_kernel,
        grid=(m // tm, n // tn),
        in_specs=[pl.BlockSpec((tm, k), lambda i, j: (i, 0)),
                  pl.BlockSpec((1, k), lambda i, j: (0, 0)),
                  pl.BlockSpec((k, tn), lambda i, j: (0, j))],
        out_specs=pl.BlockSpec((tm, tn), lambda i, j: (i, j)),
        out_shape=jax.ShapeDtypeStruct((m, n), out_dtype),
        scratch_shapes=[pltpu.VMEM((tm, k), BF16)],
        compiler_params=_cparams(("parallel", "arbitrary")),
        name="norm_matmul",
    )(x2d, g.reshape(1, k), w)


def _in_proj_kernel(x_ref, g_ref, w_ref, *out_refs, tn):
    h = _rms(x_ref[...], g_ref[...]).astype(BF16)
    col = 0
    for o_ref in out_refs:
        width = o_ref.shape[1]
        for c0 in range(0, width, tn):
            c1 = min(c0 + tn, width)
            o_ref[:, c0:c1] = jnp.dot(h, w_ref[:, col + c0:col + c1],
                                      preferred_element_type=F32).astype(o_ref.dtype)
        col += width


def _in_proj(x2d, g, w, widths, dtypes, tm=512, tn=512):
    m, k = x2d.shape
    return pl.pallas_call(
        functools.partial(_in_proj_kernel, tn=tn),
        grid=(m // tm,),
        in_specs=[pl.BlockSpec((tm, k), lambda i: (i, 0)),
                  pl.BlockSpec((1, k), lambda i: (0, 0)),
                  pl.BlockSpec(w.shape, lambda i: (0, 0))],
        out_specs=[pl.BlockSpec((tm, n), lambda i: (i, 0)) for n in widths],
        out_shape=[jax.ShapeDtypeStruct((m, n), dt) for n, dt in zip(widths, dtypes)],
        compiler_params=_cparams(("parallel",)),
        name="in_proj",
    )(x2d, g.reshape(1, k), w)


def _sb_kernel(q_ref, k_ref, v_ref, o_ref, acc_ref, *, tq):
    qi = pl.program_id(2)
    lane = lax.broadcasted_iota(jnp.int32, (tq, LANES), 1)
    row = lax.broadcasted_iota(jnp.int32, (tq, tq), 0)
    col = lax.broadcasted_iota(jnp.int32, (tq, tq), 1)
    neg_later = -((row > col).astype(BF16))
    past = col < row
    q2 = q_ref[...] * (HEAD_DIM ** -0.5)
    zero = jnp.zeros_like(q2)
    qm = (jnp.where(lane < HEAD_DIM, q2, zero), jnp.where(lane < HEAD_DIM, zero, q2))

    def block(j, newer, diagonal):
        start = pl.multiple_of(j * tq, tq)
        kb = k_ref[pl.ds(start, tq), :]
        vb = v_ref[pl.ds(start, tq), :]
        out = []
        for half in range(2):
            z2 = lax.dot_general(qm[half], kb, (((1,), (1,)), ((), ())), preferred_element_type=F32) * LOG2E
            sp = jnp.maximum(z2, 0.0) + jnp.log2(1.0 + jnp.exp2(-jnp.abs(z2)))
            if diagonal:
                sp = jnp.where(past, sp, 0.0)
            local = jnp.dot(sp.astype(BF16), neg_later, preferred_element_type=F32)
            a = jnp.exp2((z2 - sp) + (local + newer[half]))
            if diagonal:
                a = jnp.where(past, a, 0.0)
            pv = jnp.dot(a.astype(BF16), vb, preferred_element_type=F32)
            if diagonal:
                acc_ref[half] = pv
            else:
                acc_ref[half] += pv
            out.append(newer[half] - jnp.sum(sp, axis=-1, keepdims=True))
        return tuple(out)

    zeros = jnp.zeros((tq, 1), F32)
    newer = block(qi, (zeros, zeros), True)
    odd = qi % 2
    newer = lax.cond(odd == 1, lambda nw: block(qi - 1, nw, False), lambda nw: nw, newer)

    def two_blocks(p, nw):
        j = qi - 1 - odd - 2 * p
        return block(j - 1, block(j, nw, False), False)

    lax.fori_loop(0, qi // 2, two_blocks, newer)
    o_ref[...] = jnp.where(lane < HEAD_DIM, acc_ref[0], acc_ref[1]).astype(o_ref.dtype)


def _sb_attention(sb, batch, seq, tq=256):
    pairs = SB_WIDTH // LANES
    return pl.pallas_call(
        functools.partial(_sb_kernel, tq=tq),
        grid=(batch, pairs, seq // tq),
        in_specs=[pl.BlockSpec((None, tq, LANES), lambda b, p, i: (b, i, p)),
                  pl.BlockSpec((None, seq, LANES), lambda b, p, i: (b, 0, pairs + p)),
                  pl.BlockSpec((None, seq, LANES), lambda b, p, i: (b, 0, 2 * pairs + p))],
        out_specs=pl.BlockSpec((None, tq, LANES), lambda b, p, i: (b, i, p)),
        out_shape=jax.ShapeDtypeStruct((batch, seq, SB_WIDTH), BF16),
        scratch_shapes=[pltpu.VMEM((2, tq, LANES), F32)],
        compiler_params=_cparams(("parallel", "parallel", "parallel")),
        name="sb_attention",
    )(sb, sb, sb)


def _head_rms(x, g, lane):
    xx = x * x
    lo = lane < HEAD_DIM
    s_lo = jnp.sum(jnp.where(lo, xx, 0.0), axis=-1, keepdims=True)
    s_hi = jnp.sum(jnp.where(lo, 0.0, xx), axis=-1, keepdims=True)
    ms = jnp.where(lo, s_lo, s_hi) * (1.0 / HEAD_DIM)
    return x * lax.rsqrt(ms + EPS) * g


def _dil_kernel(q_ref, k_ref, v_ref, gq_ref, gk_ref, o_ref, l_ref, qn_ref, kn_ref, vn_ref,
                *, dilation, length, slopes, scale, unroll):
    nb = length // DIL_BLOCK
    pair = pl.program_id(1)
    lane_l = lax.broadcasted_iota(jnp.int32, (length, LANES), 1)
    lane_b = lax.broadcasted_iota(jnp.int32, (DIL_BLOCK, LANES), 1)
    qi = lax.broadcasted_iota(jnp.int32, (DIL_BLOCK, 2 * DIL_BLOCK), 0)
    kj = lax.broadcasted_iota(jnp.int32, (DIL_BLOCK, 2 * DIL_BLOCK), 1) - DIL_BLOCK
    gap = qi - kj
    band = (gap >= 0) & (gap <= DIL_BLOCK)
    gapf = (gap * dilation).astype(F32)
    bias_rest, bias_first = [], []
    for half in range(2):
        slope = jnp.where(pair == 0, slopes[half], slopes[2 + half])
        bias_rest.append(jnp.where(band, -(slope * gapf), NEG_BIG))
        bias_first.append(jnp.where(band & (kj >= 0), -(slope * gapf), NEG_BIG))

    zero_block = jnp.zeros((DIL_BLOCK, LANES), BF16)
    for c in range(dilation):
        rows = pl.ds(0, length) if dilation == 1 else pl.ds(c, length, stride=dilation)
        qn_ref[c] = _head_rms(q_ref[rows, :], gq_ref[...], lane_l).astype(BF16) * scale
        kn_ref[c, pl.ds(0, DIL_BLOCK), :] = zero_block
        vn_ref[c, pl.ds(0, DIL_BLOCK), :] = zero_block
        kn_ref[c, pl.ds(DIL_BLOCK, length), :] = _head_rms(k_ref[rows, :], gk_ref[...], lane_l).astype(BF16)
        vn_ref[c, pl.ds(DIL_BLOCK, length), :] = v_ref[rows, :].astype(BF16)

    def block(b):
        c, n = b // nb, b % nb
        start = pl.multiple_of(n * DIL_BLOCK, DIL_BLOCK)
        qb = qn_ref[c, pl.ds(start, DIL_BLOCK), :]
        kw = kn_ref[c, pl.ds(start, 2 * DIL_BLOCK), :]
        vw = vn_ref[c, pl.ds(start, 2 * DIL_BLOCK), :]
        outs, lses = [], []
        for half in range(2):
            in_head = (lane_b < HEAD_DIM) if half == 0 else (lane_b >= HEAD_DIM)
            qm = jnp.where(in_head, qb, jnp.zeros_like(qb))
            s = lax.dot_general(qm, kw, (((1,), (1,)), ((), ())), preferred_element_type=F32)
            s = s + jnp.where(n == 0, bias_first[half], bias_rest[half])
            m = jnp.max(s, axis=-1, keepdims=True)
            e = jnp.exp(s - m)
            denom = jnp.sum(e, axis=-1, keepdims=True)
            o = jnp.dot(e.astype(BF16), vw, preferred_element_type=F32) / denom
            outs.append(o)
            lses.append(m + jnp.log(denom))
        lo = lane_b < HEAD_DIM
        if dilation == 1:
            dst = pl.ds(start, DIL_BLOCK)
        else:
            dst = pl.ds(c + start * dilation, DIL_BLOCK, stride=dilation)
        o_ref[dst, :] = jnp.where(lo, outs[0], outs[1])
        l_ref[dst, :] = jnp.where(lo, lses[0], lses[1])

    def blocks(g, _):
        for k in range(unroll):
            block(g * unroll + k)
        return 0

    lax.fori_loop(0, dilation * nb // unroll, blocks, 0)


def _dil_group(dil, g_q2, g_k2, group, batch, seq):
    _, dilation = DIL_CONFIG[group]
    length = seq // dilation
    blocks = DIL_WIDTH // LANES
    pairs = DIL_OUT_WIDTH // LANES
    all_slopes = [2.0 ** (-ALIBI_MAX_BIAS * (i + 1) / DIL_HEADS) for i in range(DIL_HEADS)]
    slopes = tuple(float(np.float32(s)) for s in
                   all_slopes[group * DIL_HEADS_PER_GROUP:(group + 1) * DIL_HEADS_PER_GROUP])
    kern = functools.partial(_dil_kernel, dilation=dilation, length=length, slopes=slopes,
                             scale=HEAD_DIM ** -0.5, unroll=4)
    col = lambda off: (lambda b, p: (b, 0, off + group * pairs + p))
    out_sd = jax.ShapeDtypeStruct((batch, seq, DIL_OUT_WIDTH), F32)
    return pl.pallas_call(
        kern,
        grid=(batch, pairs),
        in_specs=[pl.BlockSpec((None, seq, LANES), col(0)),
                  pl.BlockSpec((None, seq, LANES), col(blocks)),
                  pl.BlockSpec((None, seq, LANES), col(2 * blocks)),
                  pl.BlockSpec((1, LANES), lambda b, p: (0, 0)),
                  pl.BlockSpec((1, LANES), lambda b, p: (0, 0))],
        out_specs=[pl.BlockSpec((None, seq, LANES), lambda b, p: (b, 0, p)),
                   pl.BlockSpec((None, seq, LANES), lambda b, p: (b, 0, p))],
        out_shape=[out_sd, out_sd],
        scratch_shapes=[pltpu.VMEM((dilation, length, LANES), BF16),
                        pltpu.VMEM((dilation, length + DIL_BLOCK, LANES), BF16),
                        pltpu.VMEM((dilation, length + DIL_BLOCK, LANES), BF16)],
        compiler_params=_cparams(("parallel", "parallel")),
        name=f"dilated_attention_r{dilation}",
    )(dil, dil, dil, g_q2, g_k2)


def _mem_kernel(q_ref, kv_ref, gq_ref, gk_ref, o_ref, *, scale):
    for h in range(MEM_HEADS):
        cols = slice(h * MEM_HEAD_DIM, (h + 1) * MEM_HEAD_DIM)
        vcols = slice(MEM_WIDTH + h * MEM_HEAD_DIM, MEM_WIDTH + (h + 1) * MEM_HEAD_DIM)
        q = _rms(q_ref[:, cols], gq_ref[...]).astype(BF16)
        k = _rms(kv_ref[:, cols], gk_ref[...]).astype(BF16)
        s = lax.dot_general(q, k, (((1,), (1,)), ((), ())), preferred_element_type=F32) * scale
        m = jnp.max(s, axis=-1, keepdims=True)
        e = jnp.exp(s - m)
        denom = jnp.sum(e, axis=-1, keepdims=True)
        o = jnp.dot(e.astype(BF16), kv_ref[:, vcols].astype(BF16), preferred_element_type=F32) / denom
        o_ref[:, cols] = o.astype(o_ref.dtype)


def _mem_attention(mq, kv, g_q, g_k, batch, seq, mem_len, tq=512):
    return pl.pallas_call(
        functools.partial(_mem_kernel, scale=MEM_HEAD_DIM ** -0.5),
        grid=(batch, seq // tq),
        in_specs=[pl.BlockSpec((None, tq, MEM_WIDTH), lambda b, i: (b, i, 0)),
                  pl.BlockSpec((None, mem_len, 2 * MEM_WIDTH), lambda b, i: (b, 0, 0)),
                  pl.BlockSpec((1, LANES), lambda b, i: (0, 0)),
                  pl.BlockSpec((1, LANES), lambda b, i: (0, 0))],
        out_specs=pl.BlockSpec((None, tq, MEM_WIDTH), lambda b, i: (b, i, 0)),
        out_shape=jax.ShapeDtypeStruct((batch, seq, MEM_WIDTH), BF16),
        compiler_params=_cparams(("parallel", "parallel")),
        name="memory_attention",
    )(mq, kv, g_q.reshape(1, LANES), g_k.reshape(1, LANES))


def _merge_kernel(x_ref, g_ref, ysb_ref, o0_ref, o1_ref, o2_ref, l0_ref, l1_ref, l2_ref, ymem_ref,
                  wg_ref, bg_ref, wsb_ref, wdil_ref, wmem_ref, wout_ref, out_ref):
    d = x_ref.shape[-1]
    x = x_ref[...]
    h = _rms(x, g_ref[...]).astype(BF16)
    l0, l1, l2 = l0_ref[...], l1_ref[...], l2_ref[...]
    m = jnp.maximum(jnp.maximum(l0, l1), l2)
    e0, e1, e2 = jnp.exp(l0 - m), jnp.exp(l1 - m), jnp.exp(l2 - m)
    inv = 1.0 / (e0 + e1 + e2)
    y_dil = ((e0 * inv) * o0_ref[...] + (e1 * inv) * o1_ref[...] + (e2 * inv) * o2_ref[...]).astype(BF16)
    merged = jnp.zeros(x.shape, F32)
    branches = ((ysb_ref[...], wsb_ref), (y_dil, wdil_ref), (ymem_ref[...], wmem_ref))
    for n, (y, w_ref) in enumerate(branches):
        pre = jnp.dot(h, wg_ref[:, n * d:(n + 1) * d], preferred_element_type=F32) + bg_ref[:, n * d:(n + 1) * d]
        gate = jax.nn.sigmoid(pre)
        merged = merged + gate * jnp.dot(y, w_ref[...], preferred_element_type=F32)
    out_ref[...] = x + jnp.dot(merged.astype(BF16), wout_ref[...], preferred_element_type=F32)


def _merge(x2d, g_mix, y_sb, o_list, l_list, y_mem, w_gate, b_gate, w_o_sb, w_o_dil, w_o_mem, w_out, tm=256):
    t, d = x2d.shape
    row = lambda w: pl.BlockSpec((tm, w), lambda i: (i, 0))
    full = lambda a: pl.BlockSpec(a.shape, lambda i: (0, 0))
    g2 = g_mix.reshape(1, d)
    b2 = b_gate.reshape(1, -1)
    args = [x2d, g2, y_sb, *o_list, *l_list, y_mem, w_gate, b2, w_o_sb, w_o_dil, w_o_mem, w_out]
    specs = [row(d), full(g2), row(SB_WIDTH)] + [row(DIL_OUT_WIDTH)] * 6 + [row(MEM_WIDTH)] + \
            [full(a) for a in (w_gate, b2, w_o_sb, w_o_dil, w_o_mem, w_out)]
    return pl.pallas_call(
        _merge_kernel,
        grid=(t // tm,),
        in_specs=specs,
        out_specs=row(d),
        out_shape=jax.ShapeDtypeStruct((t, d), F32),
        compiler_params=_cparams(("parallel",)),
        name="merge_project",
    )(*args)


def _oddeven_merge_sort_pairs(n):
    pairs = []

    def merge(lo, hi, r):
        step = r * 2
        if step < hi - lo:
            merge(lo, hi, step)
            merge(lo + r, hi, step)
            for i in range(lo + r, hi - r, step):
                pairs.append((i, i + r))
        else:
            pairs.append((lo, lo + r))

    def sort(lo, hi):
        if hi - lo >= 1:
            mid = lo + (hi - lo) // 2
            sort(lo, mid)
            sort(mid + 1, hi)
            merge(lo, hi, 1)

    sort(0, n - 1)
    return pairs


_SORT16 = _oddeven_merge_sort_pairs(PEER_TOPK)
_CAND = [(a, b) for a in range(PEER_TOPK) for b in range(PEER_TOPK) if (a + 1) * (b + 1) <= PEER_TOPK]


def _top16_sorted(sc):
    n = PEER_KEYS // 8
    v = [sc[8 * i:8 * (i + 1), :] for i in range(n)]
    for (i, j) in _SORT16:
        hi, lo = jnp.maximum(v[i], v[j]), jnp.minimum(v[i], v[j])
        v[i], v[j] = hi, lo
    for shift in (4, 2, 1):
        other = [pltpu.roll(x, shift, 0) for x in v]
        v = [jnp.maximum(v[i], other[n - 1 - i]) for i in range(n)]
        d = n // 2
        while d >= 1:
            for i in range(n):
                if (i // d) % 2 == 0:
                    hi, lo = jnp.maximum(v[i], v[i + d]), jnp.minimum(v[i], v[i + d])
                    v[i], v[i + d] = hi, lo
            d //= 2
    return v


def _exchange(v, i, j):
    a, b = v[i], v[j]
    if b is None:
        return
    if a is None:
        v[i], v[j] = b, None
    else:
        v[i], v[j] = jnp.maximum(a, b), jnp.minimum(a, b)


def _bitonic_descending(v):
    n = len(v)
    d = n // 2
    while d >= 1:
        for i in range(n):
            if (i // d) % 2 == 0:
                _exchange(v, i, i + d)
        d //= 2


def _largest16(vals):
    n = PEER_TOPK
    groups = []
    for g0 in range(0, len(vals), n):
        g = list(vals[g0:g0 + n]) + [None] * max(0, g0 + n - len(vals))
        for (i, j) in _SORT16:
            _exchange(g, i, j)
        groups.append(g)
    while len(groups) > 1:
        merged = []
        for k in range(0, len(groups) - 1, 2):
            a, b = groups[k], groups[k + 1]
            m = []
            for i in range(n):
                x, y = a[i], b[n - 1 - i]
                m.append(x if y is None else y if x is None else jnp.maximum(x, y))
            _bitonic_descending(m)
            merged.append(m)
        if len(groups) % 2:
            merged.append(groups[-1])
        groups = merged
    return groups[0]


def _peer_select_kernel(x_ref, g_ref, wq_ref, sk_ref, h2t_ref, c1_ref, e1_ref, r2_ref, e2_ref,
                        q_ref, sc_ref, top_ref, cand_ref, tau_ref, invz_ref):
    tm = x_ref.shape[0]
    h2f = _rms(x_ref[...], g_ref[...])
    h2 = h2f.astype(BF16)
    h2t_ref[...] = h2f.T.astype(BF16)
    q_ref[...] = jnp.dot(h2, wq_ref[...], preferred_element_type=F32).astype(BF16)
    for hp in range(2 * PEER_HEADS):
        sc_ref[hp] = lax.dot_general(sk_ref[hp], q_ref[:, hp * PEER_HALF:(hp + 1) * PEER_HALF],
                                     (((1,), (1,)), ((), ())), preferred_element_type=F32)

    def sort_head(h, _):
        for p in range(2):
            top = _top16_sorted(sc_ref[2 * h + p])
            for a in range(PEER_TOPK):
                top_ref[p, a, pl.ds(h, 1), :] = top[a][0:1, :]
        return 0

    lax.fori_loop(0, PEER_HEADS, sort_head, 0)

    for n, (a, b) in enumerate(_CAND):
        cand_ref[n] = top_ref[0, a] + top_ref[1, b]

    lanes = 2 * LANES
    for c0 in range(0, tm, lanes):
        top = _largest16([cand_ref[n, :, c0:c0 + lanes] for n in range(len(_CAND))])
        tau_ref[:, c0:c0 + lanes] = top[PEER_TOPK - 1]
    tau = tau_ref[...]
    top_sum = cand_ref[0]
    z = jnp.zeros((8, tm), F32)
    for n in range(len(_CAND)):
        ci = cand_ref[n]
        z = z + jnp.where(ci >= tau, jnp.exp(ci - top_sum), 0.0)
    invz_ref[...] = 0.5 / z

    def emit(h, _):
        row = pl.ds(h, 1)
        s1, s2 = sc_ref[2 * h], sc_ref[2 * h + 1]
        tau_h = tau_ref[row, :]
        cnt = jnp.zeros(s1.shape, F32)
        rank = jnp.zeros(s2.shape, F32)
        for b in range(PEER_TOPK):
            s2b = top_ref[1, b, row, :]
            cnt = jnp.where((s1 + s2b) >= tau_h, float(b + 1), cnt)
            rank = jnp.where(s2b > s2, float(b + 1), rank)
        in_top = s1 >= top_ref[0, PEER_TOPK - 1, row, :]
        c1_ref[h] = jnp.where(in_top, cnt, 0.0)
        e1_ref[h] = jnp.exp(s1 - top_ref[0, 0, row, :]) * invz_ref[row, :]
        r2_ref[h] = rank.astype(BF16)
        e2_ref[h] = jnp.exp(s2 - top_ref[1, 0, row, :]).astype(BF16)
        return 0

    lax.fori_loop(0, PEER_HEADS, emit, 0)


def _peer_select(x1, g_ffn, w_q, subkeys, tm=512):
    t, d = x1.shape
    halves = 2 * PEER_HEADS
    sk3 = subkeys.reshape(halves, PEER_KEYS, PEER_HALF)
    shape = (PEER_HEADS, PEER_KEYS, t)
    big_spec = pl.BlockSpec((PEER_HEADS, PEER_KEYS, tm), lambda i: (0, 0, i))
    return pl.pallas_call(
        _peer_select_kernel,
        grid=(t // tm,),
        in_specs=[pl.BlockSpec((tm, d), lambda i: (i, 0)),
                  pl.BlockSpec((1, d), lambda i: (0, 0)),
                  pl.BlockSpec(w_q.shape, lambda i: (0, 0)),
                  pl.BlockSpec(sk3.shape, lambda i: (0, 0, 0))],
        out_specs=[pl.BlockSpec((d, tm), lambda i: (0, i)), big_spec, big_spec, big_spec, big_spec],
        out_shape=[jax.ShapeDtypeStruct((d, t), BF16),
                   jax.ShapeDtypeStruct(shape, F32), jax.ShapeDtypeStruct(shape, F32),
                   jax.ShapeDtypeStruct(shape, BF16), jax.ShapeDtypeStruct(shape, BF16)],
        scratch_shapes=[pltpu.VMEM((tm, halves * PEER_HALF), BF16),
                        pltpu.VMEM((halves, PEER_KEYS, tm), F32),
                        pltpu.VMEM((2, PEER_TOPK, 8, tm), F32),
                        pltpu.VMEM((len(_CAND), 8, tm), F32),
                        pltpu.VMEM((8, tm), F32),
                        pltpu.VMEM((8, tm), F32)],
        compiler_params=_cparams(("parallel",)),
        name="peer_select",
    )(x1, g_ffn.reshape(1, d), w_q, sk3)


PACK = 16
DENSE_ROWS = 8


def _peer_dense_kernel(x_ref, h2t_ref, u_ref, vt_ref, c1_ref, e1_ref, r2_in_ref, e2_in_ref,
                       out_ref, acc_ref, w_ref, act_ref, r2_ref, e2_ref, *, rows, chunk):
    j = pl.program_id(1)
    tm = h2t_ref.shape[1]

    @pl.when(j == 0)
    def _():
        acc_ref[...] = jnp.zeros(acc_ref.shape, F32)
        r2_ref[...] = r2_in_ref[...]
        e2_ref[...] = e2_in_ref[...]

    act_ref[...] = jnp.dot(u_ref[...], h2t_ref[...], preferred_element_type=F32)
    sqrt_half = math.sqrt(0.5)
    n_pack = PEER_KEYS // PACK
    for c in range(tm // chunk):
        tok = slice(c * chunk, (c + 1) * chunk)
        for r0 in range(0, rows, 2):
            gate = [[jnp.zeros((PACK, chunk), BF16) for _ in range(n_pack)] for _ in range(2)]
            for h in range(PEER_HEADS):
                c1g = c1_ref[h, :, tok]
                e1g = e1_ref[h, :, tok]
                c1 = [jnp.broadcast_to(c1g[r0 + k:r0 + k + 1, :], (PACK, chunk)).astype(BF16) for k in range(2)]
                e1 = [jnp.broadcast_to(e1g[r0 + k:r0 + k + 1, :], (PACK, chunk)).astype(BF16) for k in range(2)]
                for m in range(n_pack):
                    r2 = r2_ref[h, m * PACK:(m + 1) * PACK, tok]
                    e2 = e2_ref[h, m * PACK:(m + 1) * PACK, tok]
                    for k in range(2):
                        gate[k][m] = gate[k][m] + jnp.where(r2 < c1[k], e2 * e1[k], jnp.zeros_like(e2))
            for k in range(2):
                base = (r0 + k) * PEER_KEYS
                for m in range(n_pack):
                    a = act_ref[base + m * PACK:base + (m + 1) * PACK, tok]
                    gelu2 = a * (1.0 + lax.erf(a * sqrt_half))
                    w_ref[base + m * PACK:base + (m + 1) * PACK, tok] = gate[k][m] * gelu2.astype(BF16)
    acc_ref[...] += jnp.dot(vt_ref[...], w_ref[...], preferred_element_type=F32)

    @pl.when(j == pl.num_programs(1) - 1)
    def _():
        out_ref[...] = x_ref[...] + acc_ref[...].T


def _peer_dense(x1, h2t, u, vt, c1, e1, r2, e2, tm=1024, rows=DENSE_ROWS, chunk=128):
    t, d = x1.shape
    n_exp = u.shape[0]
    tn = rows * PEER_KEYS
    big_spec = pl.BlockSpec((PEER_HEADS, PEER_KEYS, tm), lambda i, j: (0, 0, i))
    row_spec = pl.BlockSpec((PEER_HEADS, rows, tm), lambda i, j: (0, j, i))
    return pl.pallas_call(
        functools.partial(_peer_dense_kernel, rows=rows, chunk=chunk),
        grid=(t // tm, n_exp // tn),
        in_specs=[pl.BlockSpec((tm, d), lambda i, j: (i, 0)),
                  pl.BlockSpec((d, tm), lambda i, j: (0, i)),
                  pl.BlockSpec((tn, d), lambda i, j: (j, 0)),
                  pl.BlockSpec((None, d, tn), lambda i, j: (j, 0, 0)),
                  row_spec, row_spec, big_spec, big_spec],
        out_specs=pl.BlockSpec((tm, d), lambda i, j: (i, 0)),
        out_shape=jax.ShapeDtypeStruct((t, d), F32),
        scratch_shapes=[pltpu.VMEM((d, tm), F32), pltpu.VMEM((tn, tm), BF16), pltpu.VMEM((tn, tm), F32),
                        pltpu.VMEM((PEER_HEADS, PEER_KEYS, tm), BF16),
                        pltpu.VMEM((PEER_HEADS, PEER_KEYS, tm), BF16)],
        compiler_params=_cparams(("parallel", "arbitrary")),
        name="peer_dense",
    )(x1, h2t, u, vt, c1, e1, r2, e2)


def kernel(x, mem, g_mix, g_mem, w_in, w_mem_kv, g_q_dil, g_k_dil, g_q_mem, g_k_mem, w_o_sb, w_o_dil,
           w_o_mem, w_gate, b_gate, w_out, g_ffn, w_peer_q, peer_subkeys, peer_u, peer_v):
    batch, seq, d = x.shape
    mem_len = mem.shape[1]
    depth = w_in.shape[0]
    xt = x.reshape(batch * seq, d)
    for l in range(depth):
        w_in_l = w_in[l].astype(BF16)
        c_sb, c_dil = 3 * SB_WIDTH, 3 * SB_WIDTH + 3 * DIL_WIDTH
        sb, dil, mq = _in_proj(xt, g_mix[l], w_in_l, (c_sb, c_dil - c_sb, MEM_WIDTH), (BF16, F32, F32))
        kv = _norm_matmul(mem.reshape(batch * mem_len, d), g_mem[l], w_mem_kv[l].astype(BF16), F32, 512, 512)

        y_sb = _sb_attention(sb.reshape(batch, seq, c_sb), batch, seq)

        g_q2 = jnp.tile(g_q_dil[l], 2).reshape(1, LANES)
        g_k2 = jnp.tile(g_k_dil[l], 2).reshape(1, LANES)
        dil3 = dil.reshape(batch, seq, 3 * DIL_WIDTH)
        o_list, l_list = [], []
        for group in range(len(DIL_CONFIG)):
            o, lse = _dil_group(dil3, g_q2, g_k2, group, batch, seq)
            o_list.append(o.reshape(batch * seq, DIL_OUT_WIDTH))
            l_list.append(lse.reshape(batch * seq, DIL_OUT_WIDTH))

        y_mem = _mem_attention(mq.reshape(batch, seq, MEM_WIDTH), kv.reshape(batch, mem_len, 2 * MEM_WIDTH),
                               g_q_mem[l], g_k_mem[l], batch, seq, mem_len)

        x1 = _merge(xt, g_mix[l], y_sb.reshape(batch * seq, SB_WIDTH), o_list, l_list,
                    y_mem.reshape(batch * seq, MEM_WIDTH), w_gate[l].astype(BF16), b_gate[l],
                    w_o_sb[l].astype(BF16), w_o_dil[l].astype(BF16), w_o_mem[l].astype(BF16),
                    w_out[l].astype(BF16))

        h2t, c1, e1, r2, e2 = _peer_select(x1, g_ffn[l], w_peer_q[l].astype(BF16),
                                          peer_subkeys[l].astype(BF16))
        n_exp = peer_v.shape[1]
        tn = DENSE_ROWS * PEER_KEYS
        vt = peer_v[l].astype(BF16).reshape(n_exp // tn, tn, d).transpose(0, 2, 1)
        xt = _peer_dense(x1, h2t, peer_u[l].astype(BF16), vt, c1, e1, r2, e2)
    return xt.reshape(batch, seq, d)
```

```python
import functools
import math

import numpy as np
import jax
import jax.numpy as jnp
from jax import lax
from jax.experimental import pallas as pl
from jax.experimental.pallas import tpu as pltpu

F32 = jnp.float32
BF16 = jnp.bfloat16

EPS = 1e-6
HEAD_DIM = 64
LANES = 128
SB_HEADS = 8
SB_WIDTH = SB_HEADS * HEAD_DIM
DIL_CONFIG = ((128, 1), (512, 4), (2048, 16))
DIL_HEADS_PER_GROUP = 4
DIL_HEADS = len(DIL_CONFIG) * DIL_HEADS_PER_GROUP
DIL_WIDTH = DIL_HEADS * HEAD_DIM
DIL_OUT_WIDTH = DIL_HEADS_PER_GROUP * HEAD_DIM
DIL_BLOCK = 128
ALIBI_MAX_BIAS = 8.0
MEM_HEADS = 4
MEM_HEAD_DIM = 128
MEM_WIDTH = MEM_HEADS * MEM_HEAD_DIM
PEER_HEADS = 8
PEER_KEYS = 128
PEER_TOPK = 16
PEER_HALF = 128
NEG_BIG = -1e30
LOG2E = 1.4426950408889634
VMEM_LIMIT = 56 * 1024 * 1024


def _cparams(sem, flags=None):
    return pltpu.CompilerParams(dimension_semantics=sem, vmem_limit_bytes=VMEM_LIMIT, flags=flags)


def _rms(x, g):
    ms = jnp.mean(x * x, axis=-1, keepdims=True)
    return x * lax.rsqrt(ms + EPS) * g


def _norm_matmul_kernel(x_ref, g_ref, w_ref, o_ref, h_ref):
    @pl.when(pl.program_id(1) == 0)
    def _():
        h_ref[...] = _rms(x_ref[...], g_ref[...]).astype(BF16)

    o_ref[...] = jnp.dot(h_ref[...], w_ref[...], preferred_element_type=F32).astype(o_ref.dtype)


def _norm_matmul(x2d, g, w, out_dtype, tm, tn):
    m, k = x2d.shape
    n = w.shape[1]
    return pl.pallas_call(
        _norm_matmul_kernel,
        grid=(m // tm, n // tn),
        in_specs=[pl.BlockSpec((tm, k), lambda i, j: (i, 0)),
                  pl.BlockSpec((1, k), lambda i, j: (0, 0)),
                  pl.BlockSpec((k, tn), lambda i, j: (0, j))],
        out_specs=pl.BlockSpec((tm, tn), lambda i, j: (i, j)),
        out_shape=jax.ShapeDtypeStruct((m, n), out_dtype),
        scratch_shapes=[pltpu.VMEM((tm, k), BF16)],
        compiler_params=_cparams(("parallel", "arbitrary")),
        name="norm_matmul",
    )(x2d, g.reshape(1, k), w)


def _in_proj_kernel(x_ref, g_ref, w_ref, *out_refs, tn):
    h = _rms(x_ref[...], g_ref[...]).astype(BF16)
    col = 0
    for o_ref in out_refs:
        width = o_ref.shape[1]
        for c0 in range(0, width, tn):
            c1 = min(c0 + tn, width)
            o_ref[:, c0:c1] = jnp.dot(h, w_ref[:, col + c0:col + c1],
                                      preferred_element_type=F32).astype(o_ref.dtype)
        col += width


def _in_proj(x2d, g, w, widths, dtypes, tm=512, tn=512):
    m, k = x2d.shape
    return pl.pallas_call(
        functools.partial(_in_proj_kernel, tn=tn),
        grid=(m // tm,),
        in_specs=[pl.BlockSpec((tm, k), lambda i: (i, 0)),
                  pl.BlockSpec((1, k), lambda i: (0, 0)),
                  pl.BlockSpec(w.shape, lambda i: (0, 0))],
        out_specs=[pl.BlockSpec((tm, n), lambda i: (i, 0)) for n in widths],
        out_shape=[jax.ShapeDtypeStruct((m, n), dt) for n, dt in zip(widths, dtypes)],
        compiler_params=_cparams(("parallel",)),
        name="in_proj",
    )(x2d, g.reshape(1, k), w)


def _sb_kernel(q_ref, k_ref, v_ref, o_ref, acc_ref, *, tq):
    qi = pl.program_id(2)
    lane = lax.broadcasted_iota(jnp.int32, (tq, LANES), 1)
    row = lax.broadcasted_iota(jnp.int32, (tq, tq), 0)
    col = lax.broadcasted_iota(jnp.int32, (tq, tq), 1)
    neg_later = -((row > col).astype(BF16))
    past = col < row
    q2 = q_ref[...] * (HEAD_DIM ** -0.5)
    zero = jnp.zeros_like(q2)
    qm = (jnp.where(lane < HEAD_DIM, q2, zero), jnp.where(lane < HEAD_DIM, zero, q2))

    def block(j, newer, diagonal):
        start = pl.multiple_of(j * tq, tq)
        kb = k_ref[pl.ds(start, tq), :]
        vb = v_ref[pl.ds(start, tq), :]
        out = []
        for half in range(2):
            z2 = lax.dot_general(qm[half], kb, (((1,), (1,)), ((), ())), preferred_element_type=F32) * LOG2E
            sp = jnp.maximum(z2, 0.0) + jnp.log2(1.0 + jnp.exp2(-jnp.abs(z2)))
            if diagonal:
                sp = jnp.where(past, sp, 0.0)
            local = jnp.dot(sp.astype(BF16), neg_later, preferred_element_type=F32)
            a = jnp.exp2((z2 - sp) + (local + newer[half]))
            if diagonal:
                a = jnp.where(past, a, 0.0)
            pv = jnp.dot(a.astype(BF16), vb, preferred_element_type=F32)
            if diagonal:
                acc_ref[half] = pv
            else:
                acc_ref[half] += pv
            out.append(newer[half] - jnp.sum(sp, axis=-1, keepdims=True))
        return tuple(out)

    zeros = jnp.zeros((tq, 1), F32)
    newer = block(qi, (zeros, zeros), True)
    odd = qi % 2
    newer = lax.cond(odd == 1, lambda nw: block(qi - 1, nw, False), lambda nw: nw, newer)

    def two_blocks(p, nw):
        j = qi - 1 - odd - 2 * p
        return block(j - 1, block(j, nw, False), False)

    lax.fori_loop(0, qi // 2, two_blocks, newer)
    o_ref[...] = jnp.where(lane < HEAD_DIM, acc_ref[0], acc_ref[1]).astype(o_ref.dtype)


def _sb_attention(sb, batch, seq, tq=256):
    pairs = SB_WIDTH // LANES
    return pl.pallas_call(
        functools.partial(_sb_kernel, tq=tq),
        grid=(batch, pairs, seq // tq),
        in_specs=[pl.BlockSpec((None, tq, LANES), lambda b, p, i: (b, i, p)),
                  pl.BlockSpec((None, seq, LANES), lambda b, p, i: (b, 0, pairs + p)),
                  pl.BlockSpec((None, seq, LANES), lambda b, p, i: (b, 0, 2 * pairs + p))],
        out_specs=pl.BlockSpec((None, tq, LANES), lambda b, p, i: (b, i, p)),
        out_shape=jax.ShapeDtypeStruct((batch, seq, SB_WIDTH), BF16),
        scratch_shapes=[pltpu.VMEM((2, tq, LANES), F32)],
        compiler_params=_cparams(("parallel", "parallel", "parallel")),
        name="sb_attention",
    )(sb, sb, sb)


def _head_rms(x, g, lane):
    xx = x * x
    lo = lane < HEAD_DIM
    s_lo = jnp.sum(jnp.where(lo, xx, 0.0), axis=-1, keepdims=True)
    s_hi = jnp.sum(jnp.where(lo, 0.0, xx), axis=-1, keepdims=True)
    ms = jnp.where(lo, s_lo, s_hi) * (1.0 / HEAD_DIM)
    return x * lax.rsqrt(ms + EPS) * g


def _dil_kernel(q_ref, k_ref, v_ref, gq_ref, gk_ref, o_ref, l_ref, qn_ref, kn_ref, vn_ref,
                *, dilation, length, slopes, scale, unroll):
    nb = length // DIL_BLOCK
    pair = pl.program_id(1)
    lane_l = lax.broadcasted_iota(jnp.int32, (length, LANES), 1)
    lane_b = lax.broadcasted_iota(jnp.int32, (DIL_BLOCK, LANES), 1)
    qi = lax.broadcasted_iota(jnp.int32, (DIL_BLOCK, 2 * DIL_BLOCK), 0)
    kj = lax.broadcasted_iota(jnp.int32, (DIL_BLOCK, 2 * DIL_BLOCK), 1) - DIL_BLOCK
    gap = qi - kj
    band = (gap >= 0) & (gap <= DIL_BLOCK)
    gapf = (gap * dilation).astype(F32)
    bias_rest, bias_first = [], []
    for half in range(2):
        slope = jnp.where(pair == 0, slopes[half], slopes[2 + half])
        bias_rest.append(jnp.where(band, -(slope * gapf), NEG_BIG))
        bias_first.append(jnp.where(band & (kj >= 0), -(slope * gapf), NEG_BIG))

    zero_block = jnp.zeros((DIL_BLOCK, LANES), BF16)
    for c in range(dilation):
        rows = pl.ds(0, length) if dilation == 1 else pl.ds(c, length, stride=dilation)
        qn_ref[c] = _head_rms(q_ref[rows, :], gq_ref[...], lane_l).astype(BF16) * scale
        kn_ref[c, pl.ds(0, DIL_BLOCK), :] = zero_block
        vn_ref[c, pl.ds(0, DIL_BLOCK), :] = zero_block
        kn_ref[c, pl.ds(DIL_BLOCK, length), :] = _head_rms(k_ref[rows, :], gk_ref[...], lane_l).astype(BF16)
        vn_ref[c, pl.ds(DIL_BLOCK, length), :] = v_ref[rows, :].astype(BF16)

    def block(b):
        c, n = b // nb, b % nb
        start = pl.multiple_of(n * DIL_BLOCK, DIL_BLOCK)
        qb = qn_ref[c, pl.ds(start, DIL_BLOCK), :]
        kw = kn_ref[c, pl.ds(start, 2 * DIL_BLOCK), :]
        vw = vn_ref[c, pl.ds(start, 2 * DIL_BLOCK), :]
        outs, lses = [], []
        for half in range(2):
            in_head = (lane_b < HEAD_DIM) if half == 0 else (lane_b >= HEAD_DIM)
            qm = jnp.where(in_head, qb, jnp.zeros_like(qb))
            s = lax.dot_general(qm, kw, (((1,), (1,)), ((), ())), preferred_element_type=F32)
            s = s + jnp.where(n == 0, bias_first[half], bias_rest[half])
            m = jnp.max(s, axis=-1, keepdims=True)
            e = jnp.exp(s - m)
            denom = jnp.sum(e, axis=-1, keepdims=True)
            o = jnp.dot(e.astype(BF16), vw, preferred_element_type=F32) / denom
            outs.append(o)
            lses.append(m + jnp.log(denom))
        lo = lane_b < HEAD_DIM
        if dilation == 1:
            dst = pl.ds(start, DIL_BLOCK)
        else:
            dst = pl.ds(c + start * dilation, DIL_BLOCK, stride=dilation)
        o_ref[dst, :] = jnp.where(lo, outs[0], outs[1])
        l_ref[dst, :] = jnp.where(lo, lses[0], lses[1])

    def blocks(g, _):
        for k in range(unroll):
            block(g * unroll + k)
        return 0

    lax.fori_loop(0, dilation * nb // unroll, blocks, 0)


def _dil_group(dil, g_q2, g_k2, group, batch, seq):
    _, dilation = DIL_CONFIG[group]
    length = seq // dilation
    blocks = DIL_WIDTH // LANES
    pairs = DIL_OUT_WIDTH // LANES
    all_slopes = [2.0 ** (-ALIBI_MAX_BIAS * (i + 1) / DIL_HEADS) for i in range(DIL_HEADS)]
    slopes = tuple(float(np.float32(s)) for s in
                   all_slopes[group * DIL_HEADS_PER_GROUP:(group + 1) * DIL_HEADS_PER_GROUP])
    kern = functools.partial(_dil_kernel, dilation=dilation, length=length, slopes=slopes,
                             scale=HEAD_DIM ** -0.5, unroll=4)
    col = lambda off: (lambda b, p: (b, 0, off + group * pairs + p))
    out_sd = jax.ShapeDtypeStruct((batch, seq, DIL_OUT_WIDTH), F32)
    return pl.pallas_call(
        kern,
        grid=(batch, pairs),
        in_specs=[pl.BlockSpec((None, seq, LANES), col(0)),
                  pl.BlockSpec((None, seq, LANES), col(blocks)),
                  pl.BlockSpec((None, seq, LANES), col(2 * blocks)),
                  pl.BlockSpec((1, LANES), lambda b, p: (0, 0)),
                  pl.BlockSpec((1, LANES), lambda b, p: (0, 0))],
        out_specs=[pl.BlockSpec((None, seq, LANES), lambda b, p: (b, 0, p)),
                   pl.BlockSpec((None, seq, LANES), lambda b, p: (b, 0, p))],
        out_shape=[out_sd, out_sd],
        scratch_shapes=[pltpu.VMEM((dilation, length, LANES), BF16),
                        pltpu.VMEM((dilation, length + DIL_BLOCK, LANES), BF16),
                        pltpu.VMEM((dilation, length + DIL_BLOCK, LANES), BF16)],
        compiler_params=_cparams(("parallel", "parallel")),
        name=f"dilated_attention_r{dilation}",
    )(dil, dil, dil, g_q2, g_k2)


def _mem_kernel(q_ref, kv_ref, gq_ref, gk_ref, o_ref, *, scale):
    for h in range(MEM_HEADS):
        cols = slice(h * MEM_HEAD_DIM, (h + 1) * MEM_HEAD_DIM)
        vcols = slice(MEM_WIDTH + h * MEM_HEAD_DIM, MEM_WIDTH + (h + 1) * MEM_HEAD_DIM)
        q = _rms(q_ref[:, cols], gq_ref[...]).astype(BF16)
        k = _rms(kv_ref[:, cols], gk_ref[...]).astype(BF16)
        s = lax.dot_general(q, k, (((1,), (1,)), ((), ())), preferred_element_type=F32) * scale
        m = jnp.max(s, axis=-1, keepdims=True)
        e = jnp.exp(s - m)
        denom = jnp.sum(e, axis=-1, keepdims=True)
        o = jnp.dot(e.astype(BF16), kv_ref[:, vcols].astype(BF16), preferred_element_type=F32) / denom
        o_ref[:, cols] = o.astype(o_ref.dtype)


def _mem_attention(mq, kv, g_q, g_k, batch, seq, mem_len, tq=512):
    return pl.pallas_call(
        functools.partial(_mem_kernel, scale=MEM_HEAD_DIM ** -0.5),
        grid=(batch, seq // tq),
        in_specs=[pl.BlockSpec((None, tq, MEM_WIDTH), lambda b, i: (b, i, 0)),
                  pl.BlockSpec((None, mem_len, 2 * MEM_WIDTH), lambda b, i: (b, 0, 0)),
                  pl.BlockSpec((1, LANES), lambda b, i: (0, 0)),
                  pl.BlockSpec((1, LANES), lambda b, i: (0, 0))],
        out_specs=pl.BlockSpec((None, tq, MEM_WIDTH), lambda b, i: (b, i, 0)),
        out_shape=jax.ShapeDtypeStruct((batch, seq, MEM_WIDTH), BF16),
        compiler_params=_cparams(("parallel", "parallel")),
        name="memory_attention",
    )(mq, kv, g_q.reshape(1, LANES), g_k.reshape(1, LANES))


def _merge_kernel(x_ref, g_ref, ysb_ref, o0_ref, o1_ref, o2_ref, l0_ref, l1_ref, l2_ref, ymem_ref,
                  wg_ref, bg_ref, wsb_ref, wdil_ref, wmem_ref, wout_ref, out_ref):
    d = x_ref.shape[-1]
    x = x_ref[...]
    h = _rms(x, g_ref[...]).astype(BF16)
    l0, l1, l2 = l0_ref[...], l1_ref[...], l2_ref[...]
    m = jnp.maximum(jnp.maximum(l0, l1), l2)
    e0, e1, e2 = jnp.exp(l0 - m), jnp.exp(l1 - m), jnp.exp(l2 - m)
    inv = 1.0 / (e0 + e1 + e2)
    y_dil = ((e0 * inv) * o0_ref[...] + (e1 * inv) * o1_ref[...] + (e2 * inv) * o2_ref[...]).astype(BF16)
    merged = jnp.zeros(x.shape, F32)
    branches = ((ysb_ref[...], wsb_ref), (y_dil, wdil_ref), (ymem_ref[...], wmem_ref))
    for n, (y, w_ref) in enumerate(branches):
        pre = jnp.dot(h, wg_ref[:, n * d:(n + 1) * d], preferred_element_type=F32) + bg_ref[:, n * d:(n + 1) * d]
        gate = jax.nn.sigmoid(pre)
        merged = merged + gate * jnp.dot(y, w_ref[...], preferred_element_type=F32)
    out_ref[...] = x + jnp.dot(merged.astype(BF16), wout_ref[...], preferred_element_type=F32)


def _merge(x2d, g_mix, y_sb, o_list, l_list, y_mem, w_gate, b_gate, w_o_sb, w_o_dil, w_o_mem, w_out, tm=256):
    t, d = x2d.shape
    row = lambda w: pl.BlockSpec((tm, w), lambda i: (i, 0))
    full = lambda a: pl.BlockSpec(a.shape, lambda i: (0, 0))
    g2 = g_mix.reshape(1, d)
    b2 = b_gate.reshape(1, -1)
    args = [x2d, g2, y_sb, *o_list, *l_list, y_mem, w_gate, b2, w_o_sb, w_o_dil, w_o_mem, w_out]
    specs = [row(d), full(g2), row(SB_WIDTH)] + [row(DIL_OUT_WIDTH)] * 6 + [row(MEM_WIDTH)] + \
            [full(a) for a in (w_gate, b2, w_o_sb, w_o_dil, w_o_mem, w_out)]
    return pl.pallas_call(
        _merge_kernel,
        grid=(t // tm,),
        in_specs=specs,
        out_specs=row(d),
        out_shape=jax.ShapeDtypeStruct((t, d), F32),
        compiler_params=_cparams(("parallel",)),
        name="merge_project",
    )(*args)


def _oddeven_merge_sort_pairs(n):
    pairs = []

    def merge(lo, hi, r):
        step = r * 2
        if step < hi - lo:
            merge(lo, hi, step)
            merge(lo + r, hi, step)
            for i in range(lo + r, hi - r, step):
                pairs.append((i, i + r))
        else:
            pairs.append((lo, lo + r))

    def sort(lo, hi):
        if hi - lo >= 1:
            mid = lo + (hi - lo) // 2
            sort(lo, mid)
            sort(mid + 1, hi)
            merge(lo, hi, 1)

    sort(0, n - 1)
    return pairs


_SORT16 = _oddeven_merge_sort_pairs(PEER_TOPK)
_CAND = [(a, b) for a in range(PEER_TOPK) for b in range(PEER_TOPK) if (a + 1) * (b + 1) <= PEER_TOPK]


def _top16_sorted(sc):
    n = PEER_KEYS // 8
    v = [sc[8 * i:8 * (i + 1), :] for i in range(n)]
    for (i, j) in _SORT16:
        hi, lo = jnp.maximum(v[i], v[j]), jnp.minimum(v[i], v[j])
        v[i], v[j] = hi, lo
    for shift in (4, 2, 1):
        other = [pltpu.roll(x, shift, 0) for x in v]
        v = [jnp.maximum(v[i], other[n - 1 - i]) for i in range(n)]
        d = n // 2
        while d >= 1:
            for i in range(n):
                if (i // d) % 2 == 0:
                    hi, lo = jnp.maximum(v[i], v[i + d]), jnp.minimum(v[i], v[i + d])
                    v[i], v[i + d] = hi, lo
            d //= 2
    return v


def _exchange(v, i, j):
    a, b = v[i], v[j]
    if b is None:
        return
    if a is None:
        v[i], v[j] = b, None
    else:
        v[i], v[j] = jnp.maximum(a, b), jnp.minimum(a, b)


def _bitonic_descending(v):
    n = len(v)
    d = n // 2
    while d >= 1:
        for i in range(n):
            if (i // d) % 2 == 0:
                _exchange(v, i, i + d)
        d //= 2


def _largest16(vals):
    n = PEER_TOPK
    groups = []
    for g0 in range(0, len(vals), n):
        g = list(vals[g0:g0 + n]) + [None] * max(0, g0 + n - len(vals))
        for (i, j) in _SORT16:
            _exchange(g, i, j)
        groups.append(g)
    while len(groups) > 1:
        merged = []
        for k in range(0, len(groups) - 1, 2):
            a, b = groups[k], groups[k + 1]
            m = []
            for i in range(n):
                x, y = a[i], b[n - 1 - i]
                m.append(x if y is None else y if x is None else jnp.maximum(x, y))
            _bitonic_descending(m)
            merged.append(m)
        if len(groups) % 2:
            merged.append(groups[-1])
        groups = merged
    return groups[0]


def _peer_select_kernel(x_ref, g_ref, wq_ref, sk_ref, h2t_ref, c1_ref, e1_ref, r2_ref, e2_ref,
                        q_ref, sc_ref, top_ref, cand_ref, tau_ref, invz_ref):
    tm = x_ref.shape[0]
    h2f = _rms(x_ref[...], g_ref[...])
    h2 = h2f.astype(BF16)
    h2t_ref[...] = h2f.T.astype(BF16)
    q_ref[...] = jnp.dot(h2, wq_ref[...], preferred_element_type=F32).astype(BF16)
    for hp in range(2 * PEER_HEADS):
        sc_ref[hp] = lax.dot_general(sk_ref[hp], q_ref[:, hp * PEER_HALF:(hp + 1) * PEER_HALF],
                                     (((1,), (1,)), ((), ())), preferred_element_type=F32)

    def sort_head(h, _):
        for p in range(2):
            top = _top16_sorted(sc_ref[2 * h + p])
            for a in range(PEER_TOPK):
                top_ref[p, a, pl.ds(h, 1), :] = top[a][0:1, :]
        return 0

    lax.fori_loop(0, PEER_HEADS, sort_head, 0)

    for n, (a, b) in enumerate(_CAND):
        cand_ref[n] = top_ref[0, a] + top_ref[1, b]

    lanes = 2 * LANES
    for c0 in range(0, tm, lanes):
        top = _largest16([cand_ref[n, :, c0:c0 + lanes] for n in range(len(_CAND))])
        tau_ref[:, c0:c0 + lanes] = top[PEER_TOPK - 1]
    tau = tau_ref[...]
    top_sum = cand_ref[0]
    z = jnp.zeros((8, tm), F32)
    for n in range(len(_CAND)):
        ci = cand_ref[n]
        z = z + jnp.where(ci >= tau, jnp.exp(ci - top_sum), 0.0)
    invz_ref[...] = 0.5 / z

    def emit(h, _):
        row = pl.ds(h, 1)
        s1, s2 = sc_ref[2 * h], sc_ref[2 * h + 1]
        tau_h = tau_ref[row, :]
        cnt = jnp.zeros(s1.shape, F32)
        rank = jnp.zeros(s2.shape, F32)
        for b in range(PEER_TOPK):
            s2b = top_ref[1, b, row, :]
            cnt = jnp.where((s1 + s2b) >= tau_h, float(b + 1), cnt)
            rank = jnp.where(s2b > s2, float(b + 1), rank)
        in_top = s1 >= top_ref[0, PEER_TOPK - 1, row, :]
        c1_ref[h] = jnp.where(in_top, cnt, 0.0)
        e1_ref[h] = jnp.exp(s1 - top_ref[0, 0, row, :]) * invz_ref[row, :]
        r2_ref[h] = rank.astype(BF16)
        e2_ref[h] = jnp.exp(s2 - top_ref[1, 0, row, :]).astype(BF16)
        return 0

    lax.fori_loop(0, PEER_HEADS, emit, 0)


def _peer_select(x1, g_ffn, w_q, subkeys, tm=512):
    t, d = x1.shape
    halves = 2 * PEER_HEADS
    sk3 = subkeys.reshape(halves, PEER_KEYS, PEER_HALF)
    shape = (PEER_HEADS, PEER_KEYS, t)
    big_spec = pl.BlockSpec((PEER_HEADS, PEER_KEYS, tm), lambda i: (0, 0, i))
    return pl.pallas_call(
        _peer_select_kernel,
        grid=(t // tm,),
        in_specs=[pl.BlockSpec((tm, d), lambda i: (i, 0)),
                  pl.BlockSpec((1, d), lambda i: (0, 0)),
                  pl.BlockSpec(w_q.shape, lambda i: (0, 0)),
                  pl.BlockSpec(sk3.shape, lambda i: (0, 0, 0))],
        out_specs=[pl.BlockSpec((d, tm), lambda i: (0, i)), big_spec, big_spec, big_spec, big_spec],
        out_shape=[jax.ShapeDtypeStruct((d, t), BF16),
                   jax.ShapeDtypeStruct(shape, F32), jax.ShapeDtypeStruct(shape, F32),
                   jax.ShapeDtypeStruct(shape, BF16), jax.ShapeDtypeStruct(shape, BF16)],
        scratch_shapes=[pltpu.VMEM((tm, halves * PEER_HALF), BF16),
                        pltpu.VMEM((halves, PEER_KEYS, tm), F32),
                        pltpu.VMEM((2, PEER_TOPK, 8, tm), F32),
                        pltpu.VMEM((len(_CAND), 8, tm), F32),
                        pltpu.VMEM((8, tm), F32),
                        pltpu.VMEM((8, tm), F32)],
        compiler_params=_cparams(("parallel",)),
        name="peer_select",
    )(x1, g_ffn.reshape(1, d), w_q, sk3)


PACK = 16
DENSE_ROWS = 8


def _peer_dense_kernel(x_ref, h2t_ref, u_ref, vt_ref, c1_ref, e1_ref, r2_in_ref, e2_in_ref,
                       out_ref, acc_ref, w_ref, act_ref, r2_ref, e2_ref, *, rows, chunk):
    j = pl.program_id(1)
    tm = h2t_ref.shape[1]

    @pl.when(j == 0)
    def _():
        acc_ref[...] = jnp.zeros(acc_ref.shape, F32)
        r2_ref[...] = r2_in_ref[...]
        e2_ref[...] = e2_in_ref[...]

    act_ref[...] = jnp.dot(u_ref[...], h2t_ref[...], preferred_element_type=F32)
    sqrt_half = math.sqrt(0.5)
    n_pack = PEER_KEYS // PACK

    def gate_chunk(c, _):
        tok = pl.ds(pl.multiple_of(c * chunk, chunk), chunk)
        for r0 in range(0, rows, 2):
            gate = [[jnp.zeros((PACK, chunk), BF16) for _ in range(n_pack)] for _ in range(2)]
            for h in range(PEER_HEADS):
                c1g = c1_ref[h, :, tok]
                e1g = e1_ref[h, :, tok]
                c1 = [jnp.broadcast_to(c1g[r0 + k:r0 + k + 1, :], (PACK, chunk)).astype(BF16) for k in range(2)]
                e1 = [jnp.broadcast_to(e1g[r0 + k:r0 + k + 1, :], (PACK, chunk)).astype(BF16) for k in range(2)]
                for m in range(n_pack):
                    r2 = r2_ref[h, m * PACK:(m + 1) * PACK, tok]
                    e2 = e2_ref[h, m * PACK:(m + 1) * PACK, tok]
                    for k in range(2):
                        gate[k][m] = gate[k][m] + jnp.where(r2 < c1[k], e2 * e1[k], jnp.zeros_like(e2))
            for k in range(2):
                base = (r0 + k) * PEER_KEYS
                for m in range(n_pack):
                    a = act_ref[base + m * PACK:base + (m + 1) * PACK, tok]
                    gelu2 = a * (1.0 + lax.erf(a * sqrt_half))
                    w_ref[base + m * PACK:base + (m + 1) * PACK, tok] = gate[k][m] * gelu2.astype(BF16)
        return 0

    lax.fori_loop(0, tm // chunk, gate_chunk, 0)
    acc_ref[...] += jnp.dot(vt_ref[...], w_ref[...], preferred_element_type=F32)

    @pl.when(j == pl.num_programs(1) - 1)
    def _():
        out_ref[...] = x_ref[...] + acc_ref[...].T


def _peer_dense(x1, h2t, u, vt, c1, e1, r2, e2, tm=1024, rows=DENSE_ROWS, chunk=128):
    t, d = x1.shape
    n_exp = u.shape[0]
    tn = rows * PEER_KEYS
    big_spec = pl.BlockSpec((PEER_HEADS, PEER_KEYS, tm), lambda i, j: (0, 0, i))
    row_spec = pl.BlockSpec((PEER_HEADS, rows, tm), lambda i, j: (0, j, i))
    return pl.pallas_call(
        functools.partial(_peer_dense_kernel, rows=rows, chunk=chunk),
        grid=(t // tm, n_exp // tn),
        in_specs=[pl.BlockSpec((tm, d), lambda i, j: (i, 0)),
                  pl.BlockSpec((d, tm), lambda i, j: (0, i)),
                  pl.BlockSpec((tn, d), lambda i, j: (j, 0)),
                  pl.BlockSpec((None, d, tn), lambda i, j: (j, 0, 0)),
                  row_spec, row_spec, big_spec, big_spec],
        out_specs=pl.BlockSpec((tm, d), lambda i, j: (i, 0)),
        out_shape=jax.ShapeDtypeStruct((t, d), F32),
        scratch_shapes=[pltpu.VMEM((d, tm), F32), pltpu.VMEM((tn, tm), BF16), pltpu.VMEM((tn, tm), F32),
                        pltpu.VMEM((PEER_HEADS, PEER_KEYS, tm), BF16),
                        pltpu.VMEM((PEER_HEADS, PEER_KEYS, tm), BF16)],
        compiler_params=_cparams(("parallel", "arbitrary")),
        name="peer_dense",
    )(x1, h2t, u, vt, c1, e1, r2, e2)


def kernel(x, mem, g_mix, g_mem, w_in, w_mem_kv, g_q_dil, g_k_dil, g_q_mem, g_k_mem, w_o_sb, w_o_dil,
           w_o_mem, w_gate, b_gate, w_out, g_ffn, w_peer_q, peer_subkeys, peer_u, peer_v):
    batch, seq, d = x.shape
    mem_len = mem.shape[1]
    depth = w_in.shape[0]
    xt = x.reshape(batch * seq, d)
    for l in range(depth):
        w_in_l = w_in[l].astype(BF16)
        c_sb, c_dil = 3 * SB_WIDTH, 3 * SB_WIDTH + 3 * DIL_WIDTH
        sb, dil, mq = _in_proj(xt, g_mix[l], w_in_l, (c_sb, c_dil - c_sb, MEM_WIDTH), (BF16, F32, F32))
        kv = _norm_matmul(mem.reshape(batch * mem_len, d), g_mem[l], w_mem_kv[l].astype(BF16), F32, 512, 512)

        y_sb = _sb_attention(sb.reshape(batch, seq, c_sb), batch, seq)

        g_q2 = jnp.tile(g_q_dil[l], 2).reshape(1, LANES)
        g_k2 = jnp.tile(g_k_dil[l], 2).reshape(1, LANES)
        dil3 = dil.reshape(batch, seq, 3 * DIL_WIDTH)
        o_list, l_list = [], []
        for group in range(len(DIL_CONFIG)):
            o, lse = _dil_group(dil3, g_q2, g_k2, group, batch, seq)
            o_list.append(o.reshape(batch * seq, DIL_OUT_WIDTH))
            l_list.append(lse.reshape(batch * seq, DIL_OUT_WIDTH))

        y_mem = _mem_attention(mq.reshape(batch, seq, MEM_WIDTH), kv.reshape(batch, mem_len, 2 * MEM_WIDTH),
                               g_q_mem[l], g_k_mem[l], batch, seq, mem_len)

        x1 = _merge(xt, g_mix[l], y_sb.reshape(batch * seq, SB_WIDTH), o_list, l_list,
                    y_mem.reshape(batch * seq, MEM_WIDTH), w_gate[l].astype(BF16), b_gate[l],
                    w_o_sb[l].astype(BF16), w_o_dil[l].astype(BF16), w_o_mem[l].astype(BF16),
                    w_out[l].astype(BF16))

        h2t, c1, e1, r2, e2 = _peer_select(x1, g_ffn[l], w_peer_q[l].astype(BF16),
                                          peer_subkeys[l].astype(BF16))
        n_exp = peer_v.shape[1]
        tn = DENSE_ROWS * PEER_KEYS
        vt = peer_v[l].astype(BF16).reshape(n_exp // tn, tn, d).transpose(0, 2, 1)
        xt = _peer_dense(x1, h2t, peer_u[l].astype(BF16), vt, c1, e1, r2, e2)
    return xt.reshape(batch, seq, d)
```

```python
import functools
import math

import numpy as np
import jax
import jax.numpy as jnp
from jax import lax
from jax.experimental import pallas as pl
from jax.experimental.pallas import tpu as pltpu

F32 = jnp.float32
BF16 = jnp.bfloat16

EPS = 1e-6
HEAD_DIM = 64
LANES = 128
SB_HEADS = 8
SB_WIDTH = SB_HEADS * HEAD_DIM
DIL_CONFIG = ((128, 1), (512, 4), (2048, 16))
DIL_HEADS_PER_GROUP = 4
DIL_HEADS = len(DIL_CONFIG) * DIL_HEADS_PER_GROUP
DIL_WIDTH = DIL_HEADS * HEAD_DIM
DIL_OUT_WIDTH = DIL_HEADS_PER_GROUP * HEAD_DIM
DIL_BLOCK = 128
ALIBI_MAX_BIAS = 8.0
MEM_HEADS = 4
MEM_HEAD_DIM = 128
MEM_WIDTH = MEM_HEADS * MEM_HEAD_DIM
PEER_HEADS = 8
PEER_KEYS = 128
PEER_TOPK = 16
PEER_HALF = 128
NEG_BIG = -1e30
LOG2E = 1.4426950408889634
VMEM_LIMIT = 56 * 1024 * 1024


def _cparams(sem, flags=None):
    return pltpu.CompilerParams(dimension_semantics=sem, vmem_limit_bytes=VMEM_LIMIT, flags=flags)


def _rms(x, g):
    ms = jnp.mean(x * x, axis=-1, keepdims=True)
    return x * lax.rsqrt(ms + EPS) * g


def _norm_matmul_kernel(x_ref, g_ref, w_ref, o_ref, h_ref):
    @pl.when(pl.program_id(1) == 0)
    def _():
        h_ref[...] = _rms(x_ref[...], g_ref[...]).astype(BF16)

    o_ref[...] = jnp.dot(h_ref[...], w_ref[...], preferred_element_type=F32).astype(o_ref.dtype)


def _norm_matmul(x2d, g, w, out_dtype, tm, tn):
    m, k = x2d.shape
    n = w.shape[1]
    return pl.pallas_call(
        _norm_matmul_kernel,
        grid=(m // tm, n // tn),
        in_specs=[pl.BlockSpec((tm, k), lambda i, j: (i, 0)),
                  pl.BlockSpec((1, k), lambda i, j: (0, 0)),
                  pl.BlockSpec((k, tn), lambda i, j: (0, j))],
        out_specs=pl.BlockSpec((tm, tn), lambda i, j: (i, j)),
        out_shape=jax.ShapeDtypeStruct((m, n), out_dtype),
        scratch_shapes=[pltpu.VMEM((tm, k), BF16)],
        compiler_params=_cparams(("parallel", "arbitrary")),
        name="norm_matmul",
    )(x2d, g.reshape(1, k), w)


def _in_proj_kernel(x_ref, g_ref, w_ref, *out_refs, tn):
    h = _rms(x_ref[...], g_ref[...]).astype(BF16)
    col = 0
    for o_ref in out_refs:
        width = o_ref.shape[1]
        for c0 in range(0, width, tn):
            c1 = min(c0 + tn, width)
            o_ref[:, c0:c1] = jnp.dot(h, w_ref[:, col + c0:col + c1],
                                      preferred_element_type=F32).astype(o_ref.dtype)
        col += width


def _in_proj(x2d, g, w, widths, dtypes, tm=512, tn=512):
    m, k = x2d.shape
    return pl.pallas_call(
        functools.partial(_in_proj_kernel, tn=tn),
        grid=(m // tm,),
        in_specs=[pl.BlockSpec((tm, k), lambda i: (i, 0)),
                  pl.BlockSpec((1, k), lambda i: (0, 0)),
                  pl.BlockSpec(w.shape, lambda i: (0, 0))],
        out_specs=[pl.BlockSpec((tm, n), lambda i: (i, 0)) for n in widths],
        out_shape=[jax.ShapeDtypeStruct((m, n), dt) for n, dt in zip(widths, dtypes)],
        compiler_params=_cparams(("parallel",)),
        name="in_proj",
    )(x2d, g.reshape(1, k), w)


SB_GROUP = 4


def _sb_kernel(q_ref, k_ref, v_ref, o_ref, acc_ref, *, tq):
    qi = pl.program_id(2)
    width = SB_GROUP * HEAD_DIM
    head_of_lane = lax.broadcasted_iota(jnp.int32, (tq, width), 1) // HEAD_DIM
    row = lax.broadcasted_iota(jnp.int32, (tq, tq), 0)
    col = lax.broadcasted_iota(jnp.int32, (tq, tq), 1)
    neg_later = -((row > col).astype(BF16))
    past = col < row
    q_all = q_ref[...] * (HEAD_DIM ** -0.5)
    zero = jnp.zeros_like(q_all)
    qm = [jnp.where(head_of_lane == h, q_all, zero) for h in range(SB_GROUP)]

    def block(j, newer, diagonal):
        start = pl.multiple_of(j * tq, tq)
        kb = k_ref[pl.ds(start, tq), :]
        vb = v_ref[pl.ds(start, tq), :]
        weights, out = [], []
        for h in range(SB_GROUP):
            z2 = lax.dot_general(qm[h], kb, (((1,), (1,)), ((), ())), preferred_element_type=F32) * LOG2E
            sp = jnp.maximum(z2, 0.0) + jnp.log2(1.0 + jnp.exp2(-jnp.abs(z2)))
            if diagonal:
                sp = jnp.where(past, sp, 0.0)
            local = jnp.dot(sp.astype(BF16), neg_later, preferred_element_type=F32)
            a = jnp.exp2((z2 - sp) + (local + newer[h]))
            if diagonal:
                a = jnp.where(past, a, 0.0)
            weights.append(a.astype(BF16))
            out.append(newer[h] - jnp.sum(sp, axis=-1, keepdims=True))
        v_stack = jnp.concatenate([jnp.where(head_of_lane == h, vb, jnp.zeros_like(vb)) for h in range(SB_GROUP)],
                                  axis=0)
        pv = jnp.dot(jnp.concatenate(weights, axis=1), v_stack, preferred_element_type=F32)
        if diagonal:
            acc_ref[...] = pv
        else:
            acc_ref[...] += pv
        return tuple(out)

    zeros = jnp.zeros((tq, 1), F32)
    newer = block(qi, (zeros,) * SB_GROUP, True)
    odd = qi % 2
    newer = lax.cond(odd == 1, lambda nw: block(qi - 1, nw, False), lambda nw: nw, newer)

    def two_blocks(p, nw):
        j = qi - 1 - odd - 2 * p
        return block(j - 1, block(j, nw, False), False)

    lax.fori_loop(0, qi // 2, two_blocks, newer)
    o_ref[...] = acc_ref[...].astype(o_ref.dtype)


def _sb_attention(sb, batch, seq, tq=256):
    width = SB_GROUP * HEAD_DIM
    groups = SB_WIDTH // width
    return pl.pallas_call(
        functools.partial(_sb_kernel, tq=tq),
        grid=(batch, groups, seq // tq),
        in_specs=[pl.BlockSpec((None, tq, width), lambda b, p, i: (b, i, p)),
                  pl.BlockSpec((None, seq, width), lambda b, p, i: (b, 0, groups + p)),
                  pl.BlockSpec((None, seq, width), lambda b, p, i: (b, 0, 2 * groups + p))],
        out_specs=pl.BlockSpec((None, tq, width), lambda b, p, i: (b, i, p)),
        out_shape=jax.ShapeDtypeStruct((batch, seq, SB_WIDTH), BF16),
        scratch_shapes=[pltpu.VMEM((tq, width), F32)],
        compiler_params=_cparams(("parallel", "parallel", "parallel")),
        name="sb_attention",
    )(sb, sb, sb)


def _head_rms(x, g, lane):
    xx = x * x
    lo = lane < HEAD_DIM
    s_lo = jnp.sum(jnp.where(lo, xx, 0.0), axis=-1, keepdims=True)
    s_hi = jnp.sum(jnp.where(lo, 0.0, xx), axis=-1, keepdims=True)
    ms = jnp.where(lo, s_lo, s_hi) * (1.0 / HEAD_DIM)
    return x * lax.rsqrt(ms + EPS) * g


def _dil_kernel(q_ref, k_ref, v_ref, gq_ref, gk_ref, o_ref, l_ref, qn_ref, kn_ref, vn_ref,
                *, dilation, length, slopes, scale, unroll):
    nb = length // DIL_BLOCK
    pair = pl.program_id(1)
    lane_l = lax.broadcasted_iota(jnp.int32, (length, LANES), 1)
    lane_b = lax.broadcasted_iota(jnp.int32, (DIL_BLOCK, LANES), 1)
    qi = lax.broadcasted_iota(jnp.int32, (DIL_BLOCK, 2 * DIL_BLOCK), 0)
    kj = lax.broadcasted_iota(jnp.int32, (DIL_BLOCK, 2 * DIL_BLOCK), 1) - DIL_BLOCK
    gap = qi - kj
    band = (gap >= 0) & (gap <= DIL_BLOCK)
    gapf = (gap * dilation).astype(F32)
    bias_rest, bias_first = [], []
    for half in range(2):
        slope = jnp.where(pair == 0, slopes[half], slopes[2 + half])
        bias_rest.append(jnp.where(band, -(slope * gapf), NEG_BIG))
        bias_first.append(jnp.where(band & (kj >= 0), -(slope * gapf), NEG_BIG))

    zero_block = jnp.zeros((DIL_BLOCK, LANES), BF16)
    for c in range(dilation):
        rows = pl.ds(0, length) if dilation == 1 else pl.ds(c, length, stride=dilation)
        qn_ref[c] = _head_rms(q_ref[rows, :], gq_ref[...], lane_l).astype(BF16) * scale
        kn_ref[c, pl.ds(0, DIL_BLOCK), :] = zero_block
        vn_ref[c, pl.ds(0, DIL_BLOCK), :] = zero_block
        kn_ref[c, pl.ds(DIL_BLOCK, length), :] = _head_rms(k_ref[rows, :], gk_ref[...], lane_l).astype(BF16)
        vn_ref[c, pl.ds(DIL_BLOCK, length), :] = v_ref[rows, :].astype(BF16)

    def block(b):
        c, n = b // nb, b % nb
        start = pl.multiple_of(n * DIL_BLOCK, DIL_BLOCK)
        qb = qn_ref[c, pl.ds(start, DIL_BLOCK), :]
        kw = kn_ref[c, pl.ds(start, 2 * DIL_BLOCK), :]
        vw = vn_ref[c, pl.ds(start, 2 * DIL_BLOCK), :]
        outs, lses = [], []
        for half in range(2):
            in_head = (lane_b < HEAD_DIM) if half == 0 else (lane_b >= HEAD_DIM)
            qm = jnp.where(in_head, qb, jnp.zeros_like(qb))
            s = lax.dot_general(qm, kw, (((1,), (1,)), ((), ())), preferred_element_type=F32)
            s = s + jnp.where(n == 0, bias_first[half], bias_rest[half])
            m = jnp.max(s, axis=-1, keepdims=True)
            e = jnp.exp(s - m)
            denom = jnp.sum(e, axis=-1, keepdims=True)
            o = jnp.dot(e.astype(BF16), vw, preferred_element_type=F32) / denom
            outs.append(o)
            lses.append(m + jnp.log(denom))
        lo = lane_b < HEAD_DIM
        if dilation == 1:
            dst = pl.ds(start, DIL_BLOCK)
        else:
            dst = pl.ds(c + start * dilation, DIL_BLOCK, stride=dilation)
        o_ref[dst, :] = jnp.where(lo, outs[0], outs[1])
        l_ref[dst, :] = jnp.where(lo, lses[0], lses[1])

    def blocks(g, _):
        for k in range(unroll):
            block(g * unroll + k)
        return 0

    lax.fori_loop(0, dilation * nb // unroll, blocks, 0)


def _dil_group(dil, g_q2, g_k2, group, batch, seq):
    _, dilation = DIL_CONFIG[group]
    length = seq // dilation
    blocks = DIL_WIDTH // LANES
    pairs = DIL_OUT_WIDTH // LANES
    all_slopes = [2.0 ** (-ALIBI_MAX_BIAS * (i + 1) / DIL_HEADS) for i in range(DIL_HEADS)]
    slopes = tuple(float(np.float32(s)) for s in
                   all_slopes[group * DIL_HEADS_PER_GROUP:(group + 1) * DIL_HEADS_PER_GROUP])
    kern = functools.partial(_dil_kernel, dilation=dilation, length=length, slopes=slopes,
                             scale=HEAD_DIM ** -0.5, unroll=4)
    col = lambda off: (lambda b, p: (b, 0, off + group * pairs + p))
    out_sd = jax.ShapeDtypeStruct((batch, seq, DIL_OUT_WIDTH), F32)
    return pl.pallas_call(
        kern,
        grid=(batch, pairs),
        in_specs=[pl.BlockSpec((None, seq, LANES), col(0)),
                  pl.BlockSpec((None, seq, LANES), col(blocks)),
                  pl.BlockSpec((None, seq, LANES), col(2 * blocks)),
                  pl.BlockSpec((1, LANES), lambda b, p: (0, 0)),
                  pl.BlockSpec((1, LANES), lambda b, p: (0, 0))],
        out_specs=[pl.BlockSpec((None, seq, LANES), lambda b, p: (b, 0, p)),
                   pl.BlockSpec((None, seq, LANES), lambda b, p: (b, 0, p))],
        out_shape=[out_sd, out_sd],
        scratch_shapes=[pltpu.VMEM((dilation, length, LANES), BF16),
                        pltpu.VMEM((dilation, length + DIL_BLOCK, LANES), BF16),
                        pltpu.VMEM((dilation, length + DIL_BLOCK, LANES), BF16)],
        compiler_params=_cparams(("parallel", "parallel")),
        name=f"dilated_attention_r{dilation}",
    )(dil, dil, dil, g_q2, g_k2)


def _mem_kernel(q_ref, kv_ref, gq_ref, gk_ref, o_ref, *, scale):
    for h in range(MEM_HEADS):
        cols = slice(h * MEM_HEAD_DIM, (h + 1) * MEM_HEAD_DIM)
        vcols = slice(MEM_WIDTH + h * MEM_HEAD_DIM, MEM_WIDTH + (h + 1) * MEM_HEAD_DIM)
        q = _rms(q_ref[:, cols], gq_ref[...]).astype(BF16)
        k = _rms(kv_ref[:, cols], gk_ref[...]).astype(BF16)
        s = lax.dot_general(q, k, (((1,), (1,)), ((), ())), preferred_element_type=F32) * scale
        m = jnp.max(s, axis=-1, keepdims=True)
        e = jnp.exp(s - m)
        denom = jnp.sum(e, axis=-1, keepdims=True)
        o = jnp.dot(e.astype(BF16), kv_ref[:, vcols].astype(BF16), preferred_element_type=F32) / denom
        o_ref[:, cols] = o.astype(o_ref.dtype)


def _mem_attention(mq, kv, g_q, g_k, batch, seq, mem_len, tq=512):
    return pl.pallas_call(
        functools.partial(_mem_kernel, scale=MEM_HEAD_DIM ** -0.5),
        grid=(batch, seq // tq),
        in_specs=[pl.BlockSpec((None, tq, MEM_WIDTH), lambda b, i: (b, i, 0)),
                  pl.BlockSpec((None, mem_len, 2 * MEM_WIDTH), lambda b, i: (b, 0, 0)),
                  pl.BlockSpec((1, LANES), lambda b, i: (0, 0)),
                  pl.BlockSpec((1, LANES), lambda b, i: (0, 0))],
        out_specs=pl.BlockSpec((None, tq, MEM_WIDTH), lambda b, i: (b, i, 0)),
        out_shape=jax.ShapeDtypeStruct((batch, seq, MEM_WIDTH), BF16),
        compiler_params=_cparams(("parallel", "parallel")),
        name="memory_attention",
    )(mq, kv, g_q.reshape(1, LANES), g_k.reshape(1, LANES))


def _merge_kernel(x_ref, g_ref, ysb_ref, o0_ref, o1_ref, o2_ref, l0_ref, l1_ref, l2_ref, ymem_ref,
                  wg_ref, bg_ref, wsb_ref, wdil_ref, wmem_ref, wout_ref, out_ref):
    d = x_ref.shape[-1]
    x = x_ref[...]
    h = _rms(x, g_ref[...]).astype(BF16)
    l0, l1, l2 = l0_ref[...], l1_ref[...], l2_ref[...]
    m = jnp.maximum(jnp.maximum(l0, l1), l2)
    e0, e1, e2 = jnp.exp(l0 - m), jnp.exp(l1 - m), jnp.exp(l2 - m)
    inv = 1.0 / (e0 + e1 + e2)
    y_dil = ((e0 * inv) * o0_ref[...] + (e1 * inv) * o1_ref[...] + (e2 * inv) * o2_ref[...]).astype(BF16)
    merged = jnp.zeros(x.shape, F32)
    branches = ((ysb_ref[...], wsb_ref), (y_dil, wdil_ref), (ymem_ref[...], wmem_ref))
    for n, (y, w_ref) in enumerate(branches):
        pre = jnp.dot(h, wg_ref[:, n * d:(n + 1) * d], preferred_element_type=F32) + bg_ref[:, n * d:(n + 1) * d]
        gate = jax.nn.sigmoid(pre)
        merged = merged + gate * jnp.dot(y, w_ref[...], preferred_element_type=F32)
    out_ref[...] = x + jnp.dot(merged.astype(BF16), wout_ref[...], preferred_element_type=F32)


def _merge(x2d, g_mix, y_sb, o_list, l_list, y_mem, w_gate, b_gate, w_o_sb, w_o_dil, w_o_mem, w_out, tm=256):
    t, d = x2d.shape
    row = lambda w: pl.BlockSpec((tm, w), lambda i: (i, 0))
    full = lambda a: pl.BlockSpec(a.shape, lambda i: (0, 0))
    g2 = g_mix.reshape(1, d)
    b2 = b_gate.reshape(1, -1)
    args = [x2d, g2, y_sb, *o_list, *l_list, y_mem, w_gate, b2, w_o_sb, w_o_dil, w_o_mem, w_out]
    specs = [row(d), full(g2), row(SB_WIDTH)] + [row(DIL_OUT_WIDTH)] * 6 + [row(MEM_WIDTH)] + \
            [full(a) for a in (w_gate, b2, w_o_sb, w_o_dil, w_o_mem, w_out)]
    return pl.pallas_call(
        _merge_kernel,
        grid=(t // tm,),
        in_specs=specs,
        out_specs=row(d),
        out_shape=jax.ShapeDtypeStruct((t, d), F32),
        compiler_params=_cparams(("parallel",)),
        name="merge_project",
    )(*args)


def _oddeven_merge_sort_pairs(n):
    pairs = []

    def merge(lo, hi, r):
        step = r * 2
        if step < hi - lo:
            merge(lo, hi, step)
            merge(lo + r, hi, step)
            for i in range(lo + r, hi - r, step):
                pairs.append((i, i + r))
        else:
            pairs.append((lo, lo + r))

    def sort(lo, hi):
        if hi - lo >= 1:
            mid = lo + (hi - lo) // 2
            sort(lo, mid)
            sort(mid + 1, hi)
            merge(lo, hi, 1)

    sort(0, n - 1)
    return pairs


_SORT16 = _oddeven_merge_sort_pairs(PEER_TOPK)
_CAND = [(a, b) for a in range(PEER_TOPK) for b in range(PEER_TOPK) if (a + 1) * (b + 1) <= PEER_TOPK]


def _top16_sorted(sc):
    n = PEER_KEYS // 8
    v = [sc[8 * i:8 * (i + 1), :] for i in range(n)]
    for (i, j) in _SORT16:
        hi, lo = jnp.maximum(v[i], v[j]), jnp.minimum(v[i], v[j])
        v[i], v[j] = hi, lo
    for shift in (4, 2, 1):
        other = [pltpu.roll(x, shift, 0) for x in v]
        v = [jnp.maximum(v[i], other[n - 1 - i]) for i in range(n)]
        d = n // 2
        while d >= 1:
            for i in range(n):
                if (i // d) % 2 == 0:
                    hi, lo = jnp.maximum(v[i], v[i + d]), jnp.minimum(v[i], v[i + d])
                    v[i], v[i + d] = hi, lo
            d //= 2
    return v


def _exchange(v, i, j):
    a, b = v[i], v[j]
    if b is None:
        return
    if a is None:
        v[i], v[j] = b, None
    else:
        v[i], v[j] = jnp.maximum(a, b), jnp.minimum(a, b)


def _bitonic_descending(v):
    n = len(v)
    d = n // 2
    while d >= 1:
        for i in range(n):
            if (i // d) % 2 == 0:
                _exchange(v, i, i + d)
        d //= 2


def _largest16(vals):
    n = PEER_TOPK
    groups = []
    for g0 in range(0, len(vals), n):
        g = list(vals[g0:g0 + n]) + [None] * max(0, g0 + n - len(vals))
        for (i, j) in _SORT16:
            _exchange(g, i, j)
        groups.append(g)
    while len(groups) > 1:
        merged = []
        for k in range(0, len(groups) - 1, 2):
            a, b = groups[k], groups[k + 1]
            m = []
            for i in range(n):
                x, y = a[i], b[n - 1 - i]
                m.append(x if y is None else y if x is None else jnp.maximum(x, y))
            _bitonic_descending(m)
            merged.append(m)
        if len(groups) % 2:
            merged.append(groups[-1])
        groups = merged
    return groups[0]


def _peer_select_kernel(x_ref, g_ref, wq_ref, sk_ref, h2t_ref, c1_ref, e1_ref, r2_ref, e2_ref,
                        q_ref, sc_ref, top_ref, cand_ref, tau_ref, invz_ref):
    tm = x_ref.shape[0]
    h2f = _rms(x_ref[...], g_ref[...])
    h2 = h2f.astype(BF16)
    h2t_ref[...] = h2f.T.astype(BF16)
    q_ref[...] = jnp.dot(h2, wq_ref[...], preferred_element_type=F32).astype(BF16)
    for hp in range(2 * PEER_HEADS):
        sc_ref[hp] = lax.dot_general(sk_ref[hp], q_ref[:, hp * PEER_HALF:(hp + 1) * PEER_HALF],
                                     (((1,), (1,)), ((), ())), preferred_element_type=F32)

    def sort_head(h, _):
        for p in range(2):
            top = _top16_sorted(sc_ref[2 * h + p])
            for a in range(PEER_TOPK):
                top_ref[p, a, pl.ds(h, 1), :] = top[a][0:1, :]
        return 0

    lax.fori_loop(0, PEER_HEADS, sort_head, 0)

    for n, (a, b) in enumerate(_CAND):
        cand_ref[n] = top_ref[0, a] + top_ref[1, b]

    lanes = 2 * LANES
    for c0 in range(0, tm, lanes):
        top = _largest16([cand_ref[n, :, c0:c0 + lanes] for n in range(len(_CAND))])
        tau_ref[:, c0:c0 + lanes] = top[PEER_TOPK - 1]
    tau = tau_ref[...]
    top_sum = cand_ref[0]
    z = jnp.zeros((8, tm), F32)
    for n in range(len(_CAND)):
        ci = cand_ref[n]
        z = z + jnp.where(ci >= tau, jnp.exp(ci - top_sum), 0.0)
    invz_ref[...] = 0.5 / z

    def emit(h, _):
        row = pl.ds(h, 1)
        s1, s2 = sc_ref[2 * h], sc_ref[2 * h + 1]
        tau_h = tau_ref[row, :]
        cnt = jnp.zeros(s1.shape, F32)
        rank = jnp.zeros(s2.shape, F32)
        for b in range(PEER_TOPK):
            s2b = top_ref[1, b, row, :]
            cnt = jnp.where((s1 + s2b) >= tau_h, float(b + 1), cnt)
            rank = jnp.where(s2b > s2, float(b + 1), rank)
        in_top = s1 >= top_ref[0, PEER_TOPK - 1, row, :]
        c1_ref[h] = jnp.where(in_top, cnt, 0.0)
        e1_ref[h] = jnp.exp(s1 - top_ref[0, 0, row, :]) * invz_ref[row, :]
        r2_ref[h] = rank.astype(BF16)
        e2_ref[h] = jnp.exp(s2 - top_ref[1, 0, row, :]).astype(BF16)
        return 0

    lax.fori_loop(0, PEER_HEADS, emit, 0)


def _peer_select(x1, g_ffn, w_q, subkeys, tm=512):
    t, d = x1.shape
    halves = 2 * PEER_HEADS
    sk3 = subkeys.reshape(halves, PEER_KEYS, PEER_HALF)
    shape = (PEER_HEADS, PEER_KEYS, t)
    big_spec = pl.BlockSpec((PEER_HEADS, PEER_KEYS, tm), lambda i: (0, 0, i))
    return pl.pallas_call(
        _peer_select_kernel,
        grid=(t // tm,),
        in_specs=[pl.BlockSpec((tm, d), lambda i: (i, 0)),
                  pl.BlockSpec((1, d), lambda i: (0, 0)),
                  pl.BlockSpec(w_q.shape, lambda i: (0, 0)),
                  pl.BlockSpec(sk3.shape, lambda i: (0, 0, 0))],
        out_specs=[pl.BlockSpec((d, tm), lambda i: (0, i)), big_spec, big_spec, big_spec, big_spec],
        out_shape=[jax.ShapeDtypeStruct((d, t), BF16),
                   jax.ShapeDtypeStruct(shape, F32), jax.ShapeDtypeStruct(shape, F32),
                   jax.ShapeDtypeStruct(shape, BF16), jax.ShapeDtypeStruct(shape, BF16)],
        scratch_shapes=[pltpu.VMEM((tm, halves * PEER_HALF), BF16),
                        pltpu.VMEM((halves, PEER_KEYS, tm), F32),
                        pltpu.VMEM((2, PEER_TOPK, 8, tm), F32),
                        pltpu.VMEM((len(_CAND), 8, tm), F32),
                        pltpu.VMEM((8, tm), F32),
                        pltpu.VMEM((8, tm), F32)],
        compiler_params=_cparams(("parallel",)),
        name="peer_select",
    )(x1, g_ffn.reshape(1, d), w_q, sk3)


PACK = 16
DENSE_ROWS = 8


def _peer_dense_kernel(x_ref, h2t_ref, u_ref, vt_ref, c1_ref, e1_ref, r2_in_ref, e2_in_ref,
                       out_ref, acc_ref, w_ref, act_ref, r2_ref, e2_ref, *, rows, chunk):
    j = pl.program_id(1)
    tm = h2t_ref.shape[1]

    @pl.when(j == 0)
    def _():
        acc_ref[...] = jnp.zeros(acc_ref.shape, F32)
        r2_ref[...] = r2_in_ref[...]
        e2_ref[...] = e2_in_ref[...]

    act_ref[...] = jnp.dot(u_ref[...], h2t_ref[...], preferred_element_type=F32)
    sqrt_half = math.sqrt(0.5)
    n_pack = PEER_KEYS // PACK
    for c in range(tm // chunk):
        tok = slice(c * chunk, (c + 1) * chunk)
        for r0 in range(0, rows, 2):
            gate = [[jnp.zeros((PACK, chunk), BF16) for _ in range(n_pack)] for _ in range(2)]
            for h in range(PEER_HEADS):
                c1g = c1_ref[h, :, tok]
                e1g = e1_ref[h, :, tok]
                c1 = [jnp.broadcast_to(c1g[r0 + k:r0 + k + 1, :], (PACK, chunk)).astype(BF16) for k in range(2)]
                e1 = [jnp.broadcast_to(e1g[r0 + k:r0 + k + 1, :], (PACK, chunk)).astype(BF16) for k in range(2)]
                for m in range(n_pack):
                    r2 = r2_ref[h, m * PACK:(m + 1) * PACK, tok]
                    e2 = e2_ref[h, m * PACK:(m + 1) * PACK, tok]
                    for k in range(2):
                        gate[k][m] = gate[k][m] + jnp.where(r2 < c1[k], e2 * e1[k], jnp.zeros_like(e2))
            for k in range(2):
                base = (r0 + k) * PEER_KEYS
                for m in range(n_pack):
                    a = act_ref[base + m * PACK:base + (m + 1) * PACK, tok]
                    gelu2 = a * (1.0 + lax.erf(a * sqrt_half))
                    w_ref[base + m * PACK:base + (m + 1) * PACK, tok] = gate[k][m] * gelu2.astype(BF16)
    acc_ref[...] += jnp.dot(vt_ref[...], w_ref[...], preferred_element_type=F32)

    @pl.when(j == pl.num_programs(1) - 1)
    def _():
        out_ref[...] = x_ref[...] + acc_ref[...].T


def _peer_dense(x1, h2t, u, vt, c1, e1, r2, e2, tm=1024, rows=DENSE_ROWS, chunk=128):
    t, d = x1.shape
    n_exp = u.shape[0]
    tn = rows * PEER_KEYS
    big_spec = pl.BlockSpec((PEER_HEADS, PEER_KEYS, tm), lambda i, j: (0, 0, i))
    row_spec = pl.BlockSpec((PEER_HEADS, rows, tm), lambda i, j: (0, j, i))
    return pl.pallas_call(
        functools.partial(_peer_dense_kernel, rows=rows, chunk=chunk),
        grid=(t // tm, n_exp // tn),
        in_specs=[pl.BlockSpec((tm, d), lambda i, j: (i, 0)),
                  pl.BlockSpec((d, tm), lambda i, j: (0, i)),
                  pl.BlockSpec((tn, d), lambda i, j: (j, 0)),
                  pl.BlockSpec((None, d, tn), lambda i, j: (j, 0, 0)),
                  row_spec, row_spec, big_spec, big_spec],
        out_specs=pl.BlockSpec((tm, d), lambda i, j: (i, 0)),
        out_shape=jax.ShapeDtypeStruct((t, d), F32),
        scratch_shapes=[pltpu.VMEM((d, tm), F32), pltpu.VMEM((tn, tm), BF16), pltpu.VMEM((tn, tm), F32),
                        pltpu.VMEM((PEER_HEADS, PEER_KEYS, tm), BF16),
                        pltpu.VMEM((PEER_HEADS, PEER_KEYS, tm), BF16)],
        compiler_params=_cparams(("parallel", "arbitrary")),
        name="peer_dense",
    )(x1, h2t, u, vt, c1, e1, r2, e2)


def kernel(x, mem, g_mix, g_mem, w_in, w_mem_kv, g_q_dil, g_k_dil, g_q_mem, g_k_mem, w_o_sb, w_o_dil,
           w_o_mem, w_gate, b_gate, w_out, g_ffn, w_peer_q, peer_subkeys, peer_u, peer_v):
    batch, seq, d = x.shape
    mem_len = mem.shape[1]
    depth = w_in.shape[0]
    xt = x.reshape(batch * seq, d)
    for l in range(depth):
        w_in_l = w_in[l].astype(BF16)
        c_sb, c_dil = 3 * SB_WIDTH, 3 * SB_WIDTH + 3 * DIL_WIDTH
        sb, dil, mq = _in_proj(xt, g_mix[l], w_in_l, (c_sb, c_dil - c_sb, MEM_WIDTH), (BF16, F32, F32))
        kv = _norm_matmul(mem.reshape(batch * mem_len, d), g_mem[l], w_mem_kv[l].astype(BF16), F32, 512, 512)

        y_sb = _sb_attention(sb.reshape(batch, seq, c_sb), batch, seq)

        g_q2 = jnp.tile(g_q_dil[l], 2).reshape(1, LANES)
        g_k2 = jnp.tile(g_k_dil[l], 2).reshape(1, LANES)
        dil3 = dil.reshape(batch, seq, 3 * DIL_WIDTH)
        o_list, l_list = [], []
        for group in range(len(DIL_CONFIG)):
            o, lse = _dil_group(dil3, g_q2, g_k2, group, batch, seq)
            o_list.append(o.reshape(batch * seq, DIL_OUT_WIDTH))
            l_list.append(lse.reshape(batch * seq, DIL_OUT_WIDTH))

        y_mem = _mem_attention(mq.reshape(batch, seq, MEM_WIDTH), kv.reshape(batch, mem_len, 2 * MEM_WIDTH),
                               g_q_mem[l], g_k_mem[l], batch, seq, mem_len)

        x1 = _merge(xt, g_mix[l], y_sb.reshape(batch * seq, SB_WIDTH), o_list, l_list,
                    y_mem.reshape(batch * seq, MEM_WIDTH), w_gate[l].astype(BF16), b_gate[l],
                    w_o_sb[l].astype(BF16), w_o_dil[l].astype(BF16), w_o_mem[l].astype(BF16),
                    w_out[l].astype(BF16))

        h2t, c1, e1, r2, e2 = _peer_select(x1, g_ffn[l], w_peer_q[l].astype(BF16),
                                          peer_subkeys[l].astype(BF16))
        n_exp = peer_v.shape[1]
        tn = DENSE_ROWS * PEER_KEYS
        vt = peer_v[l].astype(BF16).reshape(n_exp // tn, tn, d).transpose(0, 2, 1)
        xt = _peer_dense(x1, h2t, peer_u[l].astype(BF16), vt, c1, e1, r2, e2)
    return xt.reshape(batch, seq, d)
```

```python
import functools
import math

import numpy as np
import jax
import jax.numpy as jnp
from jax import lax
from jax.experimental import pallas as pl
from jax.experimental.pallas import tpu as pltpu

F32 = jnp.float32
BF16 = jnp.bfloat16

EPS = 1e-6
HEAD_DIM = 64
LANES = 128
SB_HEADS = 8
SB_WIDTH = SB_HEADS * HEAD_DIM
DIL_CONFIG = ((128, 1), (512, 4), (2048, 16))
DIL_HEADS_PER_GROUP = 4
DIL_HEADS = len(DIL_CONFIG) * DIL_HEADS_PER_GROUP
DIL_WIDTH = DIL_HEADS * HEAD_DIM
DIL_OUT_WIDTH = DIL_HEADS_PER_GROUP * HEAD_DIM
DIL_BLOCK = 128
ALIBI_MAX_BIAS = 8.0
MEM_HEADS = 4
MEM_HEAD_DIM = 128
MEM_WIDTH = MEM_HEADS * MEM_HEAD_DIM
PEER_HEADS = 8
PEER_KEYS = 128
PEER_TOPK = 16
PEER_HALF = 128
NEG_BIG = -1e30
LOG2E = 1.4426950408889634
VMEM_LIMIT = 56 * 1024 * 1024


def _cparams(sem, flags=None):
    return pltpu.CompilerParams(dimension_semantics=sem, vmem_limit_bytes=VMEM_LIMIT, flags=flags)


def _rms(x, g):
    ms = jnp.mean(x * x, axis=-1, keepdims=True)
    return x * lax.rsqrt(ms + EPS) * g


def _norm_matmul_kernel(x_ref, g_ref, w_ref, o_ref, h_ref):
    @pl.when(pl.program_id(1) == 0)
    def _():
        h_ref[...] = _rms(x_ref[...], g_ref[...]).astype(BF16)

    o_ref[...] = jnp.dot(h_ref[...], w_ref[...], preferred_element_type=F32).astype(o_ref.dtype)


def _norm_matmul(x2d, g, w, out_dtype, tm, tn):
    m, k = x2d.shape
    n = w.shape[1]
    return pl.pallas_call(
        _norm_matmul_kernel,
        grid=(m // tm, n // tn),
        in_specs=[pl.BlockSpec((tm, k), lambda i, j: (i, 0)),
                  pl.BlockSpec((1, k), lambda i, j: (0, 0)),
                  pl.BlockSpec((k, tn), lambda i, j: (0, j))],
        out_specs=pl.BlockSpec((tm, tn), lambda i, j: (i, j)),
        out_shape=jax.ShapeDtypeStruct((m, n), out_dtype),
        scratch_shapes=[pltpu.VMEM((tm, k), BF16)],
        compiler_params=_cparams(("parallel", "arbitrary")),
        name="norm_matmul",
    )(x2d, g.reshape(1, k), w)


def _in_proj_kernel(x_ref, g_ref, w_ref, *out_refs, tn):
    h = _rms(x_ref[...], g_ref[...]).astype(BF16)
    col = 0
    for o_ref in out_refs:
        width = o_ref.shape[1]
        for c0 in range(0, width, tn):
            c1 = min(c0 + tn, width)
            o_ref[:, c0:c1] = jnp.dot(h, w_ref[:, col + c0:col + c1],
                                      preferred_element_type=F32).astype(o_ref.dtype)
        col += width


def _in_proj(x2d, g, w, widths, dtypes, tm=512, tn=512):
    m, k = x2d.shape
    return pl.pallas_call(
        functools.partial(_in_proj_kernel, tn=tn),
        grid=(m // tm,),
        in_specs=[pl.BlockSpec((tm, k), lambda i: (i, 0)),
                  pl.BlockSpec((1, k), lambda i: (0, 0)),
                  pl.BlockSpec(w.shape, lambda i: (0, 0))],
        out_specs=[pl.BlockSpec((tm, n), lambda i: (i, 0)) for n in widths],
        out_shape=[jax.ShapeDtypeStruct((m, n), dt) for n, dt in zip(widths, dtypes)],
        compiler_params=_cparams(("parallel",)),
        name="in_proj",
    )(x2d, g.reshape(1, k), w)


SB_GROUP = 4


def _sb_kernel(q_ref, k_ref, v_ref, o_ref, acc_ref, *, tq):
    qi = pl.program_id(2)
    width = SB_GROUP * HEAD_DIM
    head_of_lane = lax.broadcasted_iota(jnp.int32, (tq, width), 1) // HEAD_DIM
    row = lax.broadcasted_iota(jnp.int32, (tq, tq), 0)
    col = lax.broadcasted_iota(jnp.int32, (tq, tq), 1)
    neg_later = -((row > col).astype(BF16))
    past = col < row
    q_all = q_ref[...] * (HEAD_DIM ** -0.5)
    zero = jnp.zeros_like(q_all)
    qm = [jnp.where(head_of_lane == h, q_all, zero) for h in range(SB_GROUP)]

    def block(j, newer, diagonal):
        start = pl.multiple_of(j * tq, tq)
        kb = k_ref[pl.ds(start, tq), :]
        vb = v_ref[pl.ds(start, tq), :]
        weights, out = [], []
        for h in range(SB_GROUP):
            z2 = lax.dot_general(qm[h], kb, (((1,), (1,)), ((), ())), preferred_element_type=F32) * LOG2E
            sp = jnp.maximum(z2, 0.0) + jnp.log2(1.0 + jnp.exp2(-jnp.abs(z2)))
            if diagonal:
                sp = jnp.where(past, sp, 0.0)
            local = jnp.dot(sp.astype(BF16), neg_later, preferred_element_type=F32)
            a = jnp.exp2((z2 - sp) + (local + newer[h]))
            if diagonal:
                a = jnp.where(past, a, 0.0)
            weights.append(a.astype(BF16))
            out.append(newer[h] - jnp.sum(sp, axis=-1, keepdims=True))
        v_stack = jnp.concatenate([jnp.where(head_of_lane == h, vb, jnp.zeros_like(vb)) for h in range(SB_GROUP)],
                                  axis=0)
        pv = jnp.dot(jnp.concatenate(weights, axis=1), v_stack, preferred_element_type=F32)
        if diagonal:
            acc_ref[...] = pv
        else:
            acc_ref[...] += pv
        return tuple(out)

    zeros = jnp.zeros((tq, 1), F32)
    newer = block(qi, (zeros,) * SB_GROUP, True)
    odd = qi % 2
    newer = lax.cond(odd == 1, lambda nw: block(qi - 1, nw, False), lambda nw: nw, newer)

    def two_blocks(p, nw):
        j = qi - 1 - odd - 2 * p
        return block(j - 1, block(j, nw, False), False)

    lax.fori_loop(0, qi // 2, two_blocks, newer)
    o_ref[...] = acc_ref[...].astype(o_ref.dtype)


def _sb_attention(sb, batch, seq, tq=256):
    width = SB_GROUP * HEAD_DIM
    groups = SB_WIDTH // width
    return pl.pallas_call(
        functools.partial(_sb_kernel, tq=tq),
        grid=(batch, groups, seq // tq),
        in_specs=[pl.BlockSpec((None, tq, width), lambda b, p, i: (b, i, p)),
                  pl.BlockSpec((None, seq, width), lambda b, p, i: (b, 0, groups + p)),
                  pl.BlockSpec((None, seq, width), lambda b, p, i: (b, 0, 2 * groups + p))],
        out_specs=pl.BlockSpec((None, tq, width), lambda b, p, i: (b, i, p)),
        out_shape=jax.ShapeDtypeStruct((batch, seq, SB_WIDTH), BF16),
        scratch_shapes=[pltpu.VMEM((tq, width), F32)],
        compiler_params=_cparams(("parallel", "parallel", "parallel")),
        name="sb_attention",
    )(sb, sb, sb)


def _head_rms(x, g, lane):
    xx = x * x
    lo = lane < HEAD_DIM
    s_lo = jnp.sum(jnp.where(lo, xx, 0.0), axis=-1, keepdims=True)
    s_hi = jnp.sum(jnp.where(lo, 0.0, xx), axis=-1, keepdims=True)
    ms = jnp.where(lo, s_lo, s_hi) * (1.0 / HEAD_DIM)
    return x * lax.rsqrt(ms + EPS) * g


def _dil_kernel(q_ref, k_ref, v_ref, gq_ref, gk_ref, o_ref, l_ref, qn_ref, kn_ref, vn_ref,
                *, dilation, length, slopes, scale, unroll):
    nb = length // DIL_BLOCK
    pair = pl.program_id(1)
    lane_l = lax.broadcasted_iota(jnp.int32, (length, LANES), 1)
    lane_b = lax.broadcasted_iota(jnp.int32, (DIL_BLOCK, LANES), 1)
    qi = lax.broadcasted_iota(jnp.int32, (DIL_BLOCK, 2 * DIL_BLOCK), 0)
    kj = lax.broadcasted_iota(jnp.int32, (DIL_BLOCK, 2 * DIL_BLOCK), 1) - DIL_BLOCK
    gap = qi - kj
    band = (gap >= 0) & (gap <= DIL_BLOCK)
    gapf = (gap * dilation).astype(F32)
    bias_rest, bias_first = [], []
    for half in range(2):
        slope = jnp.where(pair == 0, slopes[half], slopes[2 + half])
        bias_rest.append(jnp.where(band, -(slope * gapf), NEG_BIG))
        bias_first.append(jnp.where(band & (kj >= 0), -(slope * gapf), NEG_BIG))

    zero_block = jnp.zeros((DIL_BLOCK, LANES), BF16)
    for c in range(dilation):
        rows = pl.ds(0, length) if dilation == 1 else pl.ds(c, length, stride=dilation)
        qn_ref[c] = _head_rms(q_ref[rows, :], gq_ref[...], lane_l).astype(BF16) * scale
        kn_ref[c, pl.ds(0, DIL_BLOCK), :] = zero_block
        vn_ref[c, pl.ds(0, DIL_BLOCK), :] = zero_block
        kn_ref[c, pl.ds(DIL_BLOCK, length), :] = _head_rms(k_ref[rows, :], gk_ref[...], lane_l).astype(BF16)
        vn_ref[c, pl.ds(DIL_BLOCK, length), :] = v_ref[rows, :].astype(BF16)

    def block(b):
        c, n = b // nb, b % nb
        start = pl.multiple_of(n * DIL_BLOCK, DIL_BLOCK)
        qb = qn_ref[c, pl.ds(start, DIL_BLOCK), :]
        kw = kn_ref[c, pl.ds(start, 2 * DIL_BLOCK), :]
        vw = vn_ref[c, pl.ds(start, 2 * DIL_BLOCK), :]
        outs, lses = [], []
        for half in range(2):
            in_head = (lane_b < HEAD_DIM) if half == 0 else (lane_b >= HEAD_DIM)
            qm = jnp.where(in_head, qb, jnp.zeros_like(qb))
            s = lax.dot_general(qm, kw, (((1,), (1,)), ((), ())), preferred_element_type=F32)
            s = s + jnp.where(n == 0, bias_first[half], bias_rest[half])
            m = jnp.max(s, axis=-1, keepdims=True)
            e = jnp.exp(s - m)
            denom = jnp.sum(e, axis=-1, keepdims=True)
            o = jnp.dot(e.astype(BF16), vw, preferred_element_type=F32) / denom
            outs.append(o)
            lses.append(m + jnp.log(denom))
        lo = lane_b < HEAD_DIM
        if dilation == 1:
            dst = pl.ds(start, DIL_BLOCK)
        else:
            dst = pl.ds(c + start * dilation, DIL_BLOCK, stride=dilation)
        o_ref[dst, :] = jnp.where(lo, outs[0], outs[1])
        l_ref[dst, :] = jnp.where(lo, lses[0], lses[1])

    def blocks(g, _):
        for k in range(unroll):
            block(g * unroll + k)
        return 0

    lax.fori_loop(0, dilation * nb // unroll, blocks, 0)


def _dil_group(dil, g_q2, g_k2, group, batch, seq):
    _, dilation = DIL_CONFIG[group]
    length = seq // dilation
    blocks = DIL_WIDTH // LANES
    pairs = DIL_OUT_WIDTH // LANES
    all_slopes = [2.0 ** (-ALIBI_MAX_BIAS * (i + 1) / DIL_HEADS) for i in range(DIL_HEADS)]
    slopes = tuple(float(np.float32(s)) for s in
                   all_slopes[group * DIL_HEADS_PER_GROUP:(group + 1) * DIL_HEADS_PER_GROUP])
    kern = functools.partial(_dil_kernel, dilation=dilation, length=length, slopes=slopes,
                             scale=HEAD_DIM ** -0.5, unroll=4)
    col = lambda off: (lambda b, p: (b, 0, off + group * pairs + p))
    out_sd = jax.ShapeDtypeStruct((batch, seq, DIL_OUT_WIDTH), F32)
    return pl.pallas_call(
        kern,
        grid=(batch, pairs),
        in_specs=[pl.BlockSpec((None, seq, LANES), col(0)),
                  pl.BlockSpec((None, seq, LANES), col(blocks)),
                  pl.BlockSpec((None, seq, LANES), col(2 * blocks)),
                  pl.BlockSpec((1, LANES), lambda b, p: (0, 0)),
                  pl.BlockSpec((1, LANES), lambda b, p: (0, 0))],
        out_specs=[pl.BlockSpec((None, seq, LANES), lambda b, p: (b, 0, p)),
                   pl.BlockSpec((None, seq, LANES), lambda b, p: (b, 0, p))],
        out_shape=[out_sd, out_sd],
        scratch_shapes=[pltpu.VMEM((dilation, length, LANES), BF16),
                        pltpu.VMEM((dilation, length + DIL_BLOCK, LANES), BF16),
                        pltpu.VMEM((dilation, length + DIL_BLOCK, LANES), BF16)],
        compiler_params=_cparams(("parallel", "parallel")),
        name=f"dilated_attention_r{dilation}",
    )(dil, dil, dil, g_q2, g_k2)


def _mem_kernel(q_ref, kv_ref, gq_ref, gk_ref, o_ref, *, scale):
    for h in range(MEM_HEADS):
        cols = slice(h * MEM_HEAD_DIM, (h + 1) * MEM_HEAD_DIM)
        vcols = slice(MEM_WIDTH + h * MEM_HEAD_DIM, MEM_WIDTH + (h + 1) * MEM_HEAD_DIM)
        q = _rms(q_ref[:, cols], gq_ref[...]).astype(BF16)
        k = _rms(kv_ref[:, cols], gk_ref[...]).astype(BF16)
        s = lax.dot_general(q, k, (((1,), (1,)), ((), ())), preferred_element_type=F32) * scale
        m = jnp.max(s, axis=-1, keepdims=True)
        e = jnp.exp(s - m)
        denom = jnp.sum(e, axis=-1, keepdims=True)
        o = jnp.dot(e.astype(BF16), kv_ref[:, vcols].astype(BF16), preferred_element_type=F32) / denom
        o_ref[:, cols] = o.astype(o_ref.dtype)


def _mem_attention(mq, kv, g_q, g_k, batch, seq, mem_len, tq=512):
    return pl.pallas_call(
        functools.partial(_mem_kernel, scale=MEM_HEAD_DIM ** -0.5),
        grid=(batch, seq // tq),
        in_specs=[pl.BlockSpec((None, tq, MEM_WIDTH), lambda b, i: (b, i, 0)),
                  pl.BlockSpec((None, mem_len, 2 * MEM_WIDTH), lambda b, i: (b, 0, 0)),
                  pl.BlockSpec((1, LANES), lambda b, i: (0, 0)),
                  pl.BlockSpec((1, LANES), lambda b, i: (0, 0))],
        out_specs=pl.BlockSpec((None, tq, MEM_WIDTH), lambda b, i: (b, i, 0)),
        out_shape=jax.ShapeDtypeStruct((batch, seq, MEM_WIDTH), BF16),
        compiler_params=_cparams(("parallel", "parallel")),
        name="memory_attention",
    )(mq, kv, g_q.reshape(1, LANES), g_k.reshape(1, LANES))


def _merge_kernel(x_ref, g_ref, ysb_ref, o0_ref, o1_ref, o2_ref, l0_ref, l1_ref, l2_ref, ymem_ref,
                  wg_ref, bg_ref, wsb_ref, wdil_ref, wmem_ref, wout_ref, out_ref):
    d = x_ref.shape[-1]
    x = x_ref[...]
    h = _rms(x, g_ref[...]).astype(BF16)
    l0, l1, l2 = l0_ref[...], l1_ref[...], l2_ref[...]
    m = jnp.maximum(jnp.maximum(l0, l1), l2)
    e0, e1, e2 = jnp.exp(l0 - m), jnp.exp(l1 - m), jnp.exp(l2 - m)
    inv = 1.0 / (e0 + e1 + e2)
    y_dil = ((e0 * inv) * o0_ref[...] + (e1 * inv) * o1_ref[...] + (e2 * inv) * o2_ref[...]).astype(BF16)
    merged = jnp.zeros(x.shape, F32)
    branches = ((ysb_ref[...], wsb_ref), (y_dil, wdil_ref), (ymem_ref[...], wmem_ref))
    for n, (y, w_ref) in enumerate(branches):
        pre = jnp.dot(h, wg_ref[:, n * d:(n + 1) * d], preferred_element_type=F32) + bg_ref[:, n * d:(n + 1) * d]
        gate = jax.nn.sigmoid(pre)
        merged = merged + gate * jnp.dot(y, w_ref[...], preferred_element_type=F32)
    out_ref[...] = x + jnp.dot(merged.astype(BF16), wout_ref[...], preferred_element_type=F32)


def _merge(x2d, g_mix, y_sb, o_list, l_list, y_mem, w_gate, b_gate, w_o_sb, w_o_dil, w_o_mem, w_out, tm=256):
    t, d = x2d.shape
    row = lambda w: pl.BlockSpec((tm, w), lambda i: (i, 0))
    full = lambda a: pl.BlockSpec(a.shape, lambda i: (0, 0))
    g2 = g_mix.reshape(1, d)
    b2 = b_gate.reshape(1, -1)
    args = [x2d, g2, y_sb, *o_list, *l_list, y_mem, w_gate, b2, w_o_sb, w_o_dil, w_o_mem, w_out]
    specs = [row(d), full(g2), row(SB_WIDTH)] + [row(DIL_OUT_WIDTH)] * 6 + [row(MEM_WIDTH)] + \
            [full(a) for a in (w_gate, b2, w_o_sb, w_o_dil, w_o_mem, w_out)]
    return pl.pallas_call(
        _merge_kernel,
        grid=(t // tm,),
        in_specs=specs,
        out_specs=row(d),
        out_shape=jax.ShapeDtypeStruct((t, d), F32),
        compiler_params=_cparams(("parallel",)),
        name="merge_project",
    )(*args)


def _oddeven_merge_sort_pairs(n):
    pairs = []

    def merge(lo, hi, r):
        step = r * 2
        if step < hi - lo:
            merge(lo, hi, step)
            merge(lo + r, hi, step)
            for i in range(lo + r, hi - r, step):
                pairs.append((i, i + r))
        else:
            pairs.append((lo, lo + r))

    def sort(lo, hi):
        if hi - lo >= 1:
            mid = lo + (hi - lo) // 2
            sort(lo, mid)
            sort(mid + 1, hi)
            merge(lo, hi, 1)

    sort(0, n - 1)
    return pairs


_SORT16 = _oddeven_merge_sort_pairs(PEER_TOPK)
_CAND = [(a, b) for a in range(PEER_TOPK) for b in range(PEER_TOPK) if (a + 1) * (b + 1) <= PEER_TOPK]


def _top16_sorted(sc):
    n = PEER_KEYS // 8
    v = [sc[8 * i:8 * (i + 1), :] for i in range(n)]
    for (i, j) in _SORT16:
        hi, lo = jnp.maximum(v[i], v[j]), jnp.minimum(v[i], v[j])
        v[i], v[j] = hi, lo
    for shift in (4, 2, 1):
        other = [pltpu.roll(x, shift, 0) for x in v]
        v = [jnp.maximum(v[i], other[n - 1 - i]) for i in range(n)]
        d = n // 2
        while d >= 1:
            for i in range(n):
                if (i // d) % 2 == 0:
                    hi, lo = jnp.maximum(v[i], v[i + d]), jnp.minimum(v[i], v[i + d])
                    v[i], v[i + d] = hi, lo
            d //= 2
    return v


def _exchange(v, i, j):
    a, b = v[i], v[j]
    if b is None:
        return
    if a is None:
        v[i], v[j] = b, None
    else:
        v[i], v[j] = jnp.maximum(a, b), jnp.minimum(a, b)


def _bitonic_descending(v):
    n = len(v)
    d = n // 2
    while d >= 1:
        for i in range(n):
            if (i // d) % 2 == 0:
                _exchange(v, i, i + d)
        d //= 2


def _largest16(vals):
    n = PEER_TOPK
    groups = []
    for g0 in range(0, len(vals), n):
        g = list(vals[g0:g0 + n]) + [None] * max(0, g0 + n - len(vals))
        for (i, j) in _SORT16:
            _exchange(g, i, j)
        groups.append(g)
    while len(groups) > 1:
        merged = []
        for k in range(0, len(groups) - 1, 2):
            a, b = groups[k], groups[k + 1]
            m = []
            for i in range(n):
                x, y = a[i], b[n - 1 - i]
                m.append(x if y is None else y if x is None else jnp.maximum(x, y))
            _bitonic_descending(m)
            merged.append(m)
        if len(groups) % 2:
            merged.append(groups[-1])
        groups = merged
    return groups[0]


def _peer_select_kernel(x_ref, g_ref, wq_ref, sk_ref, h2t_ref, c1_ref, e1_ref, r2_ref, e2_ref,
                        q_ref, sc_ref, top_ref, cand_ref, tau_ref, invz_ref):
    tm = x_ref.shape[0]
    h2f = _rms(x_ref[...], g_ref[...])
    h2 = h2f.astype(BF16)
    h2t_ref[...] = h2f.T.astype(BF16)
    q_ref[...] = jnp.dot(h2, wq_ref[...], preferred_element_type=F32).astype(BF16)
    for hp in range(2 * PEER_HEADS):
        sc_ref[hp] = lax.dot_general(sk_ref[hp], q_ref[:, hp * PEER_HALF:(hp + 1) * PEER_HALF],
                                     (((1,), (1,)), ((), ())), preferred_element_type=F32)

    def sort_head(h, _):
        for p in range(2):
            top = _top16_sorted(sc_ref[2 * h + p])
            for a in range(PEER_TOPK):
                top_ref[p, a, pl.ds(h, 1), :] = top[a][0:1, :]
        return 0

    lax.fori_loop(0, PEER_HEADS, sort_head, 0)

    for n, (a, b) in enumerate(_CAND):
        cand_ref[n] = top_ref[0, a] + top_ref[1, b]

    lanes = 2 * LANES
    for c0 in range(0, tm, lanes):
        top = _largest16([cand_ref[n, :, c0:c0 + lanes] for n in range(len(_CAND))])
        tau_ref[:, c0:c0 + lanes] = top[PEER_TOPK - 1]
    tau = tau_ref[...]
    top_sum = cand_ref[0]
    z = jnp.zeros((8, tm), F32)
    for n in range(len(_CAND)):
        ci = cand_ref[n]
        z = z + jnp.where(ci >= tau, jnp.exp(ci - top_sum), 0.0)
    invz_ref[...] = 0.5 / z

    def emit(h, _):
        row = pl.ds(h, 1)
        s1, s2 = sc_ref[2 * h], sc_ref[2 * h + 1]
        tau_h = tau_ref[row, :]
        cnt = jnp.zeros(s1.shape, F32)
        rank = jnp.zeros(s2.shape, F32)
        for b in range(PEER_TOPK):
            s2b = top_ref[1, b, row, :]
            cnt = jnp.where((s1 + s2b) >= tau_h, float(b + 1), cnt)
            rank = jnp.where(s2b > s2, float(b + 1), rank)
        in_top = s1 >= top_ref[0, PEER_TOPK - 1, row, :]
        c1_ref[h] = jnp.where(in_top, cnt, 0.0)
        e1_ref[h] = jnp.exp(s1 - top_ref[0, 0, row, :]) * invz_ref[row, :]
        r2_ref[h] = rank.astype(BF16)
        e2_ref[h] = jnp.exp(s2 - top_ref[1, 0, row, :]).astype(BF16)
        return 0

    lax.fori_loop(0, PEER_HEADS, emit, 0)


def _peer_select(x1, g_ffn, w_q, subkeys, tm=512):
    t, d = x1.shape
    halves = 2 * PEER_HEADS
    sk3 = subkeys.reshape(halves, PEER_KEYS, PEER_HALF)
    shape = (PEER_HEADS, PEER_KEYS, t)
    big_spec = pl.BlockSpec((PEER_HEADS, PEER_KEYS, tm), lambda i: (0, 0, i))
    return pl.pallas_call(
        _peer_select_kernel,
        grid=(t // tm,),
        in_specs=[pl.BlockSpec((tm, d), lambda i: (i, 0)),
                  pl.BlockSpec((1, d), lambda i: (0, 0)),
                  pl.BlockSpec(w_q.shape, lambda i: (0, 0)),
                  pl.BlockSpec(sk3.shape, lambda i: (0, 0, 0))],
        out_specs=[pl.BlockSpec((d, tm), lambda i: (0, i)), big_spec, big_spec, big_spec, big_spec],
        out_shape=[jax.ShapeDtypeStruct((d, t), BF16),
                   jax.ShapeDtypeStruct(shape, F32), jax.ShapeDtypeStruct(shape, F32),
                   jax.ShapeDtypeStruct(shape, BF16), jax.ShapeDtypeStruct(shape, BF16)],
        scratch_shapes=[pltpu.VMEM((tm, halves * PEER_HALF), BF16),
                        pltpu.VMEM((halves, PEER_KEYS, tm), F32),
                        pltpu.VMEM((2, PEER_TOPK, 8, tm), F32),
                        pltpu.VMEM((len(_CAND), 8, tm), F32),
                        pltpu.VMEM((8, tm), F32),
                        pltpu.VMEM((8, tm), F32)],
        compiler_params=_cparams(("parallel",)),
        name="peer_select",
    )(x1, g_ffn.reshape(1, d), w_q, sk3)


PACK = 16
DENSE_ROWS = 8


def _peer_dense_kernel(x_ref, h2t_ref, u_ref, vt_ref, c1_ref, e1_ref, r2_in_ref, e2_in_ref,
                       out_ref, acc_ref, w_ref, act_a_ref, act_b_ref, r2_ref, e2_ref, *, rows, chunk):
    j = pl.program_id(1)

    @pl.when(j == 0)
    def _():
        acc_ref[...] = jnp.zeros(acc_ref.shape, F32)
        act_b_ref[...] = jnp.zeros(act_b_ref.shape, F32)
        r2_ref[...] = r2_in_ref[...]
        e2_ref[...] = e2_in_ref[...]

    step = functools.partial(_peer_dense_step, h2t_ref, u_ref, vt_ref, c1_ref, e1_ref, acc_ref, w_ref,
                             r2_ref, e2_ref, rows=rows, chunk=chunk)

    @pl.when(j % 2 == 0)
    def _():
        step(act_a_ref, act_b_ref)

    @pl.when(j % 2 == 1)
    def _():
        step(act_b_ref, act_a_ref)

    @pl.when(j == pl.num_programs(1) - 1)
    def _():
        out_ref[...] = x_ref[...] + acc_ref[...].T


def _zero_after(value):
    bits = lax.bitcast_convert_type(jnp.max(value.astype(F32)), jnp.int32)
    return lax.shift_right_logical(lax.shift_right_logical(bits, 16), 16)


def _peer_dense_step(h2t_ref, u_ref, vt_ref, c1_ref, e1_ref, acc_ref, w_ref, r2_ref, e2_ref,
                     act_new_ref, act_ref, *, rows, chunk):
    tm = h2t_ref.shape[1]
    sqrt_half = math.sqrt(0.5)
    n_pack = PEER_KEYS // PACK
    tile = 2 * LANES
    pieces = [(q, n) for q in range(rows // 2) for n in range(tm // tile)]
    sub_chunk = 0
    for c in range(tm // chunk):
        tok = slice(c * chunk, (c + 1) * chunk)
        for r0 in range(0, rows, 2):
            gate = [[jnp.zeros((PACK, chunk), BF16) for _ in range(n_pack)] for _ in range(2)]
            for h in range(PEER_HEADS):
                c1g = c1_ref[h, :, tok]
                e1g = e1_ref[h, :, tok]
                c1 = [jnp.broadcast_to(c1g[r0 + k:r0 + k + 1, :], (PACK, chunk)).astype(BF16) for k in range(2)]
                e1 = [jnp.broadcast_to(e1g[r0 + k:r0 + k + 1, :], (PACK, chunk)).astype(BF16) for k in range(2)]
                for m in range(n_pack):
                    r2 = r2_ref[h, m * PACK:(m + 1) * PACK, tok]
                    e2 = e2_ref[h, m * PACK:(m + 1) * PACK, tok]
                    for k in range(2):
                        gate[k][m] = gate[k][m] + jnp.where(r2 < c1[k], e2 * e1[k], jnp.zeros_like(e2))
            produced = None
            for k in range(2):
                base = (r0 + k) * PEER_KEYS
                for m in range(n_pack):
                    a = act_ref[base + m * PACK:base + (m + 1) * PACK, tok]
                    gelu2 = a * (1.0 + lax.erf(a * sqrt_half))
                    weight = gate[k][m] * gelu2.astype(BF16)
                    w_ref[base + m * PACK:base + (m + 1) * PACK, tok] = weight
                    produced = weight if produced is None else jnp.maximum(produced, weight)
            if sub_chunk % 2 == 0 and sub_chunk // 2 < len(pieces):
                q, n = pieces[sub_chunk // 2]
                first = pl.multiple_of(q * tile + _zero_after(produced), tile)
                act_new_ref[q * tile:(q + 1) * tile, n * tile:(n + 1) * tile] = jnp.dot(
                    u_ref[pl.ds(first, tile), :], h2t_ref[:, n * tile:(n + 1) * tile],
                    preferred_element_type=F32)
            sub_chunk += 1
    acc_ref[...] += jnp.dot(vt_ref[...], w_ref[...], preferred_element_type=F32)


def _peer_dense(x1, h2t, u, vt, c1, e1, r2, e2, tm=512, rows=DENSE_ROWS, chunk=128):
    t, d = x1.shape
    n_exp = u.shape[0]
    tn = rows * PEER_KEYS
    groups = n_exp // tn
    gated = lambda j: jnp.maximum(j - 1, 0)
    big_spec = pl.BlockSpec((PEER_HEADS, PEER_KEYS, tm), lambda i, j: (0, 0, i))
    row_spec = pl.BlockSpec((PEER_HEADS, rows, tm), lambda i, j: (0, gated(j), i))
    return pl.pallas_call(
        functools.partial(_peer_dense_kernel, rows=rows, chunk=chunk),
        grid=(t // tm, groups + 1),
        in_specs=[pl.BlockSpec((tm, d), lambda i, j: (i, 0)),
                  pl.BlockSpec((d, tm), lambda i, j: (0, i)),
                  pl.BlockSpec((tn, d), lambda i, j: (jnp.minimum(j, groups - 1), 0)),
                  pl.BlockSpec((None, d, tn), lambda i, j: (gated(j), 0, 0)),
                  row_spec, row_spec, big_spec, big_spec],
        out_specs=pl.BlockSpec((tm, d), lambda i, j: (i, 0)),
        out_shape=jax.ShapeDtypeStruct((t, d), F32),
        scratch_shapes=[pltpu.VMEM((d, tm), F32), pltpu.VMEM((tn, tm), BF16),
                        pltpu.VMEM((tn, tm), F32), pltpu.VMEM((tn, tm), F32),
                        pltpu.VMEM((PEER_HEADS, PEER_KEYS, tm), BF16),
                        pltpu.VMEM((PEER_HEADS, PEER_KEYS, tm), BF16)],
        compiler_params=_cparams(("parallel", "arbitrary")),
        name="peer_dense",
    )(x1, h2t, u, vt, c1, e1, r2, e2)


def kernel(x, mem, g_mix, g_mem, w_in, w_mem_kv, g_q_dil, g_k_dil, g_q_mem, g_k_mem, w_o_sb, w_o_dil,
           w_o_mem, w_gate, b_gate, w_out, g_ffn, w_peer_q, peer_subkeys, peer_u, peer_v):
    batch, seq, d = x.shape
    mem_len = mem.shape[1]
    depth = w_in.shape[0]
    xt = x.reshape(batch * seq, d)
    for l in range(depth):
        w_in_l = w_in[l].astype(BF16)
        c_sb, c_dil = 3 * SB_WIDTH, 3 * SB_WIDTH + 3 * DIL_WIDTH
        sb, dil, mq = _in_proj(xt, g_mix[l], w_in_l, (c_sb, c_dil - c_sb, MEM_WIDTH), (BF16, F32, F32))
        kv = _norm_matmul(mem.reshape(batch * mem_len, d), g_mem[l], w_mem_kv[l].astype(BF16), F32, 512, 512)

        y_sb = _sb_attention(sb.reshape(batch, seq, c_sb), batch, seq)

        g_q2 = jnp.tile(g_q_dil[l], 2).reshape(1, LANES)
        g_k2 = jnp.tile(g_k_dil[l], 2).reshape(1, LANES)
        dil3 = dil.reshape(batch, seq, 3 * DIL_WIDTH)
        o_list, l_list = [], []
        for group in range(len(DIL_CONFIG)):
            o, lse = _dil_group(dil3, g_q2, g_k2, group, batch, seq)
            o_list.append(o.reshape(batch * seq, DIL_OUT_WIDTH))
            l_list.append(lse.reshape(batch * seq, DIL_OUT_WIDTH))

        y_mem = _mem_attention(mq.reshape(batch, seq, MEM_WIDTH), kv.reshape(batch, mem_len, 2 * MEM_WIDTH),
                               g_q_mem[l], g_k_mem[l], batch, seq, mem_len)

        x1 = _merge(xt, g_mix[l], y_sb.reshape(batch * seq, SB_WIDTH), o_list, l_list,
                    y_mem.reshape(batch * seq, MEM_WIDTH), w_gate[l].astype(BF16), b_gate[l],
                    w_o_sb[l].astype(BF16), w_o_dil[l].astype(BF16), w_o_mem[l].astype(BF16),
                    w_out[l].astype(BF16))

        h2t, c1, e1, r2, e2 = _peer_select(x1, g_ffn[l], w_peer_q[l].astype(BF16),
                                          peer_subkeys[l].astype(BF16))
        n_exp = peer_v.shape[1]
        tn = DENSE_ROWS * PEER_KEYS
        vt = peer_v[l].astype(BF16).reshape(n_exp // tn, tn, d).transpose(0, 2, 1)
        xt = _peer_dense(x1, h2t, peer_u[l].astype(BF16), vt, c1, e1, r2, e2)
    return xt.reshape(batch, seq, d)
```

```python
import functools
import math

import numpy as np
import jax
import jax.numpy as jnp
from jax import lax
from jax.experimental import pallas as pl
from jax.experimental.pallas import tpu as pltpu

F32 = jnp.float32
BF16 = jnp.bfloat16

EPS = 1e-6
HEAD_DIM = 64
LANES = 128
SB_HEADS = 8
SB_WIDTH = SB_HEADS * HEAD_DIM
DIL_CONFIG = ((128, 1), (512, 4), (2048, 16))
DIL_HEADS_PER_GROUP = 4
DIL_HEADS = len(DIL_CONFIG) * DIL_HEADS_PER_GROUP
DIL_WIDTH = DIL_HEADS * HEAD_DIM
DIL_OUT_WIDTH = DIL_HEADS_PER_GROUP * HEAD_DIM
DIL_BLOCK = 128
ALIBI_MAX_BIAS = 8.0
MEM_HEADS = 4
MEM_HEAD_DIM = 128
MEM_WIDTH = MEM_HEADS * MEM_HEAD_DIM
PEER_HEADS = 8
PEER_KEYS = 128
PEER_TOPK = 16
PEER_HALF = 128
NEG_BIG = -1e30
LOG2E = 1.4426950408889634
VMEM_LIMIT = 56 * 1024 * 1024


def _cparams(sem, flags=None):
    return pltpu.CompilerParams(dimension_semantics=sem, vmem_limit_bytes=VMEM_LIMIT, flags=flags)


def _rms(x, g):
    ms = jnp.mean(x * x, axis=-1, keepdims=True)
    return x * lax.rsqrt(ms + EPS) * g


def _norm_matmul_kernel(x_ref, g_ref, w_ref, o_ref, h_ref):
    @pl.when(pl.program_id(1) == 0)
    def _():
        h_ref[...] = _rms(x_ref[...], g_ref[...]).astype(BF16)

    o_ref[...] = jnp.dot(h_ref[...], w_ref[...], preferred_element_type=F32).astype(o_ref.dtype)


def _norm_matmul(x2d, g, w, out_dtype, tm, tn):
    m, k = x2d.shape
    n = w.shape[1]
    return pl.pallas_call(
        _norm_matmul_kernel,
        grid=(m // tm, n // tn),
        in_specs=[pl.BlockSpec((tm, k), lambda i, j: (i, 0)),
                  pl.BlockSpec((1, k), lambda i, j: (0, 0)),
                  pl.BlockSpec((k, tn), lambda i, j: (0, j))],
        out_specs=pl.BlockSpec((tm, tn), lambda i, j: (i, j)),
        out_shape=jax.ShapeDtypeStruct((m, n), out_dtype),
        scratch_shapes=[pltpu.VMEM((tm, k), BF16)],
        compiler_params=_cparams(("parallel", "arbitrary")),
        name="norm_matmul",
    )(x2d, g.reshape(1, k), w)


def _in_proj_kernel(x_ref, g_ref, w_ref, *out_refs, tn):
    h = _rms(x_ref[...], g_ref[...]).astype(BF16)
    col = 0
    for o_ref in out_refs:
        width = o_ref.shape[1]
        for c0 in range(0, width, tn):
            c1 = min(c0 + tn, width)
            o_ref[:, c0:c1] = jnp.dot(h, w_ref[:, col + c0:col + c1],
                                      preferred_element_type=F32).astype(o_ref.dtype)
        col += width


def _in_proj(x2d, g, w, widths, dtypes, tm=512, tn=512):
    m, k = x2d.shape
    return pl.pallas_call(
        functools.partial(_in_proj_kernel, tn=tn),
        grid=(m // tm,),
        in_specs=[pl.BlockSpec((tm, k), lambda i: (i, 0)),
                  pl.BlockSpec((1, k), lambda i: (0, 0)),
                  pl.BlockSpec(w.shape, lambda i: (0, 0))],
        out_specs=[pl.BlockSpec((tm, n), lambda i: (i, 0)) for n in widths],
        out_shape=[jax.ShapeDtypeStruct((m, n), dt) for n, dt in zip(widths, dtypes)],
        compiler_params=_cparams(("parallel",)),
        name="in_proj",
    )(x2d, g.reshape(1, k), w)


SB_GROUP = 4


def _sb_kernel(q_ref, k_ref, v_ref, o_ref, acc_ref, *, tq):
    qi = pl.program_id(2)
    width = SB_GROUP * HEAD_DIM
    head_of_lane = lax.broadcasted_iota(jnp.int32, (tq, width), 1) // HEAD_DIM
    row = lax.broadcasted_iota(jnp.int32, (tq, tq), 0)
    col = lax.broadcasted_iota(jnp.int32, (tq, tq), 1)
    neg_later = -((row > col).astype(BF16))
    past = col < row
    q_all = q_ref[...] * (HEAD_DIM ** -0.5)
    zero = jnp.zeros_like(q_all)
    qm = [jnp.where(head_of_lane == h, q_all, zero) for h in range(SB_GROUP)]

    def block(j, newer, diagonal):
        start = pl.multiple_of(j * tq, tq)
        kb = k_ref[pl.ds(start, tq), :]
        vb = v_ref[pl.ds(start, tq), :]
        weights, out = [], []
        for h in range(SB_GROUP):
            z2 = lax.dot_general(qm[h], kb, (((1,), (1,)), ((), ())), preferred_element_type=F32) * LOG2E
            sp = jnp.maximum(z2, 0.0) + jnp.log2(1.0 + jnp.exp2(-jnp.abs(z2)))
            if diagonal:
                sp = jnp.where(past, sp, 0.0)
            local = jnp.dot(sp.astype(BF16), neg_later, preferred_element_type=F32)
            a = jnp.exp2((z2 - sp) + (local + newer[h]))
            if diagonal:
                a = jnp.where(past, a, 0.0)
            weights.append(a.astype(BF16))
            out.append(newer[h] - jnp.sum(sp, axis=-1, keepdims=True))
        v_stack = jnp.concatenate([jnp.where(head_of_lane == h, vb, jnp.zeros_like(vb)) for h in range(SB_GROUP)],
                                  axis=0)
        pv = jnp.dot(jnp.concatenate(weights, axis=1), v_stack, preferred_element_type=F32)
        if diagonal:
            acc_ref[...] = pv
        else:
            acc_ref[...] += pv
        return tuple(out)

    zeros = jnp.zeros((tq, 1), F32)
    odd = qi % 2
    start = (zeros,) * SB_GROUP
    newer = lax.cond(odd == 1,
                     lambda: block(qi - 1, block(qi, start, True), False),
                     lambda: block(qi, start, True))

    def two_blocks(p, nw):
        j = qi - 1 - odd - 2 * p
        return block(j - 1, block(j, nw, False), False)

    lax.fori_loop(0, qi // 2, two_blocks, newer)
    o_ref[...] = acc_ref[...].astype(o_ref.dtype)


def _sb_attention(sb, batch, seq, tq=256):
    width = SB_GROUP * HEAD_DIM
    groups = SB_WIDTH // width
    return pl.pallas_call(
        functools.partial(_sb_kernel, tq=tq),
        grid=(batch, groups, seq // tq),
        in_specs=[pl.BlockSpec((None, tq, width), lambda b, p, i: (b, i, p)),
                  pl.BlockSpec((None, seq, width), lambda b, p, i: (b, 0, groups + p)),
                  pl.BlockSpec((None, seq, width), lambda b, p, i: (b, 0, 2 * groups + p))],
        out_specs=pl.BlockSpec((None, tq, width), lambda b, p, i: (b, i, p)),
        out_shape=jax.ShapeDtypeStruct((batch, seq, SB_WIDTH), BF16),
        scratch_shapes=[pltpu.VMEM((tq, width), F32)],
        compiler_params=_cparams(("parallel", "parallel", "parallel")),
        name="sb_attention",
    )(sb, sb, sb)


def _head_rms(x, g, lane):
    xx = x * x
    lo = lane < HEAD_DIM
    s_lo = jnp.sum(jnp.where(lo, xx, 0.0), axis=-1, keepdims=True)
    s_hi = jnp.sum(jnp.where(lo, 0.0, xx), axis=-1, keepdims=True)
    ms = jnp.where(lo, s_lo, s_hi) * (1.0 / HEAD_DIM)
    return x * lax.rsqrt(ms + EPS) * g


def _dil_kernel(q_ref, k_ref, v_ref, gq_ref, gk_ref, o_ref, l_ref, qn_ref, kn_ref, vn_ref,
                *, dilation, length, slopes, scale, unroll):
    nb = length // DIL_BLOCK
    pair = pl.program_id(1)
    lane_l = lax.broadcasted_iota(jnp.int32, (length, LANES), 1)
    lane_b = lax.broadcasted_iota(jnp.int32, (DIL_BLOCK, LANES), 1)
    qi = lax.broadcasted_iota(jnp.int32, (DIL_BLOCK, 2 * DIL_BLOCK), 0)
    kj = lax.broadcasted_iota(jnp.int32, (DIL_BLOCK, 2 * DIL_BLOCK), 1) - DIL_BLOCK
    gap = qi - kj
    band = (gap >= 0) & (gap <= DIL_BLOCK)
    gapf = (gap * dilation).astype(F32)
    bias_rest, bias_first = [], []
    for half in range(2):
        slope = jnp.where(pair == 0, slopes[half], slopes[2 + half])
        bias_rest.append(jnp.where(band, -(slope * gapf), NEG_BIG))
        bias_first.append(jnp.where(band & (kj >= 0), -(slope * gapf), NEG_BIG))

    zero_block = jnp.zeros((DIL_BLOCK, LANES), BF16)
    for c in range(dilation):
        rows = pl.ds(0, length) if dilation == 1 else pl.ds(c, length, stride=dilation)
        qn_ref[c] = _head_rms(q_ref[rows, :], gq_ref[...], lane_l).astype(BF16) * scale
        kn_ref[c, pl.ds(0, DIL_BLOCK), :] = zero_block
        vn_ref[c, pl.ds(0, DIL_BLOCK), :] = zero_block
        kn_ref[c, pl.ds(DIL_BLOCK, length), :] = _head_rms(k_ref[rows, :], gk_ref[...], lane_l).astype(BF16)
        vn_ref[c, pl.ds(DIL_BLOCK, length), :] = v_ref[rows, :].astype(BF16)

    def block(b):
        c, n = b // nb, b % nb
        start = pl.multiple_of(n * DIL_BLOCK, DIL_BLOCK)
        qb = qn_ref[c, pl.ds(start, DIL_BLOCK), :]
        kw = kn_ref[c, pl.ds(start, 2 * DIL_BLOCK), :]
        vw = vn_ref[c, pl.ds(start, 2 * DIL_BLOCK), :]
        outs, lses = [], []
        for half in range(2):
            in_head = (lane_b < HEAD_DIM) if half == 0 else (lane_b >= HEAD_DIM)
            qm = jnp.where(in_head, qb, jnp.zeros_like(qb))
            s = lax.dot_general(qm, kw, (((1,), (1,)), ((), ())), preferred_element_type=F32)
            s = s + jnp.where(n == 0, bias_first[half], bias_rest[half])
            m = jnp.max(s, axis=-1, keepdims=True)
            e = jnp.exp(s - m)
            denom = jnp.sum(e, axis=-1, keepdims=True)
            o = jnp.dot(e.astype(BF16), vw, preferred_element_type=F32) / denom
            outs.append(o)
            lses.append(m + jnp.log(denom))
        lo = lane_b < HEAD_DIM
        if dilation == 1:
            dst = pl.ds(start, DIL_BLOCK)
        else:
            dst = pl.ds(c + start * dilation, DIL_BLOCK, stride=dilation)
        o_ref[dst, :] = jnp.where(lo, outs[0], outs[1])
        l_ref[dst, :] = jnp.where(lo, lses[0], lses[1])

    def blocks(g, _):
        for k in range(unroll):
            block(g * unroll + k)
        return 0

    lax.fori_loop(0, dilation * nb // unroll, blocks, 0)


def _dil_group(dil, g_q2, g_k2, group, batch, seq):
    _, dilation = DIL_CONFIG[group]
    length = seq // dilation
    blocks = DIL_WIDTH // LANES
    pairs = DIL_OUT_WIDTH // LANES
    all_slopes = [2.0 ** (-ALIBI_MAX_BIAS * (i + 1) / DIL_HEADS) for i in range(DIL_HEADS)]
    slopes = tuple(float(np.float32(s)) for s in
                   all_slopes[group * DIL_HEADS_PER_GROUP:(group + 1) * DIL_HEADS_PER_GROUP])
    kern = functools.partial(_dil_kernel, dilation=dilation, length=length, slopes=slopes,
                             scale=HEAD_DIM ** -0.5, unroll=8)
    col = lambda off: (lambda b, p: (b, 0, off + group * pairs + p))
    out_sd = jax.ShapeDtypeStruct((batch, seq, DIL_OUT_WIDTH), F32)
    return pl.pallas_call(
        kern,
        grid=(batch, pairs),
        in_specs=[pl.BlockSpec((None, seq, LANES), col(0)),
                  pl.BlockSpec((None, seq, LANES), col(blocks)),
                  pl.BlockSpec((None, seq, LANES), col(2 * blocks)),
                  pl.BlockSpec((1, LANES), lambda b, p: (0, 0)),
                  pl.BlockSpec((1, LANES), lambda b, p: (0, 0))],
        out_specs=[pl.BlockSpec((None, seq, LANES), lambda b, p: (b, 0, p)),
                   pl.BlockSpec((None, seq, LANES), lambda b, p: (b, 0, p))],
        out_shape=[out_sd, out_sd],
        scratch_shapes=[pltpu.VMEM((dilation, length, LANES), BF16),
                        pltpu.VMEM((dilation, length + DIL_BLOCK, LANES), BF16),
                        pltpu.VMEM((dilation, length + DIL_BLOCK, LANES), BF16)],
        compiler_params=_cparams(("parallel", "parallel")),
        name=f"dilated_attention_r{dilation}",
    )(dil, dil, dil, g_q2, g_k2)


def _mem_kernel(q_ref, kv_ref, gq_ref, gk_ref, o_ref, *, scale):
    for h in range(MEM_HEADS):
        cols = slice(h * MEM_HEAD_DIM, (h + 1) * MEM_HEAD_DIM)
        vcols = slice(MEM_WIDTH + h * MEM_HEAD_DIM, MEM_WIDTH + (h + 1) * MEM_HEAD_DIM)
        q = _rms(q_ref[:, cols], gq_ref[...]).astype(BF16)
        k = _rms(kv_ref[:, cols], gk_ref[...]).astype(BF16)
        s = lax.dot_general(q, k, (((1,), (1,)), ((), ())), preferred_element_type=F32) * scale
        m = jnp.max(s, axis=-1, keepdims=True)
        e = jnp.exp(s - m)
        denom = jnp.sum(e, axis=-1, keepdims=True)
        o = jnp.dot(e.astype(BF16), kv_ref[:, vcols].astype(BF16), preferred_element_type=F32) / denom
        o_ref[:, cols] = o.astype(o_ref.dtype)


def _mem_attention(mq, kv, g_q, g_k, batch, seq, mem_len, tq=512):
    return pl.pallas_call(
        functools.partial(_mem_kernel, scale=MEM_HEAD_DIM ** -0.5),
        grid=(batch, seq // tq),
        in_specs=[pl.BlockSpec((None, tq, MEM_WIDTH), lambda b, i: (b, i, 0)),
                  pl.BlockSpec((None, mem_len, 2 * MEM_WIDTH), lambda b, i: (b, 0, 0)),
                  pl.BlockSpec((1, LANES), lambda b, i: (0, 0)),
                  pl.BlockSpec((1, LANES), lambda b, i: (0, 0))],
        out_specs=pl.BlockSpec((None, tq, MEM_WIDTH), lambda b, i: (b, i, 0)),
        out_shape=jax.ShapeDtypeStruct((batch, seq, MEM_WIDTH), BF16),
        compiler_params=_cparams(("parallel", "parallel")),
        name="memory_attention",
    )(mq, kv, g_q.reshape(1, LANES), g_k.reshape(1, LANES))


def _merge_kernel(x_ref, g_ref, ysb_ref, o0_ref, o1_ref, o2_ref, l0_ref, l1_ref, l2_ref, ymem_ref,
                  wg_ref, bg_ref, wsb_ref, wdil_ref, wmem_ref, wout_ref, out_ref):
    d = x_ref.shape[-1]
    x = x_ref[...]
    h = _rms(x, g_ref[...]).astype(BF16)
    l0, l1, l2 = l0_ref[...], l1_ref[...], l2_ref[...]
    m = jnp.maximum(jnp.maximum(l0, l1), l2)
    e0, e1, e2 = jnp.exp(l0 - m), jnp.exp(l1 - m), jnp.exp(l2 - m)
    inv = 1.0 / (e0 + e1 + e2)
    y_dil = ((e0 * inv) * o0_ref[...] + (e1 * inv) * o1_ref[...] + (e2 * inv) * o2_ref[...]).astype(BF16)
    merged = jnp.zeros(x.shape, F32)
    branches = ((ysb_ref[...], wsb_ref), (y_dil, wdil_ref), (ymem_ref[...], wmem_ref))
    for n, (y, w_ref) in enumerate(branches):
        pre = jnp.dot(h, wg_ref[:, n * d:(n + 1) * d], preferred_element_type=F32) + bg_ref[:, n * d:(n + 1) * d]
        gate = jax.nn.sigmoid(pre)
        merged = merged + gate * jnp.dot(y, w_ref[...], preferred_element_type=F32)
    out_ref[...] = x + jnp.dot(merged.astype(BF16), wout_ref[...], preferred_element_type=F32)


def _merge(x2d, g_mix, y_sb, o_list, l_list, y_mem, w_gate, b_gate, w_o_sb, w_o_dil, w_o_mem, w_out, tm=256):
    t, d = x2d.shape
    row = lambda w: pl.BlockSpec((tm, w), lambda i: (i, 0))
    full = lambda a: pl.BlockSpec(a.shape, lambda i: (0, 0))
    g2 = g_mix.reshape(1, d)
    b2 = b_gate.reshape(1, -1)
    args = [x2d, g2, y_sb, *o_list, *l_list, y_mem, w_gate, b2, w_o_sb, w_o_dil, w_o_mem, w_out]
    specs = [row(d), full(g2), row(SB_WIDTH)] + [row(DIL_OUT_WIDTH)] * 6 + [row(MEM_WIDTH)] + \
            [full(a) for a in (w_gate, b2, w_o_sb, w_o_dil, w_o_mem, w_out)]
    return pl.pallas_call(
        _merge_kernel,
        grid=(t // tm,),
        in_specs=specs,
        out_specs=row(d),
        out_shape=jax.ShapeDtypeStruct((t, d), F32),
        compiler_params=_cparams(("parallel",)),
        name="merge_project",
    )(*args)


def _oddeven_merge_sort_pairs(n):
    pairs = []

    def merge(lo, hi, r):
        step = r * 2
        if step < hi - lo:
            merge(lo, hi, step)
            merge(lo + r, hi, step)
            for i in range(lo + r, hi - r, step):
                pairs.append((i, i + r))
        else:
            pairs.append((lo, lo + r))

    def sort(lo, hi):
        if hi - lo >= 1:
            mid = lo + (hi - lo) // 2
            sort(lo, mid)
            sort(mid + 1, hi)
            merge(lo, hi, 1)

    sort(0, n - 1)
    return pairs


_SORT16 = _oddeven_merge_sort_pairs(PEER_TOPK)
_CAND = [(a, b) for a in range(PEER_TOPK) for b in range(PEER_TOPK) if (a + 1) * (b + 1) <= PEER_TOPK]


def _top16_sorted(sc):
    n = PEER_KEYS // 8
    v = [sc[8 * i:8 * (i + 1), :] for i in range(n)]
    for (i, j) in _SORT16:
        hi, lo = jnp.maximum(v[i], v[j]), jnp.minimum(v[i], v[j])
        v[i], v[j] = hi, lo
    for shift in (4, 2, 1):
        other = [pltpu.roll(x, shift, 0) for x in v]
        v = [jnp.maximum(v[i], other[n - 1 - i]) for i in range(n)]
        d = n // 2
        while d >= 1:
            for i in range(n):
                if (i // d) % 2 == 0:
                    hi, lo = jnp.maximum(v[i], v[i + d]), jnp.minimum(v[i], v[i + d])
                    v[i], v[i + d] = hi, lo
            d //= 2
    return v


def _exchange(v, i, j):
    a, b = v[i], v[j]
    if b is None:
        return
    if a is None:
        v[i], v[j] = b, None
    else:
        v[i], v[j] = jnp.maximum(a, b), jnp.minimum(a, b)


def _bitonic_descending(v):
    n = len(v)
    d = n // 2
    while d >= 1:
        for i in range(n):
            if (i // d) % 2 == 0:
                _exchange(v, i, i + d)
        d //= 2


def _largest16(vals):
    n = PEER_TOPK
    groups = []
    for g0 in range(0, len(vals), n):
        g = list(vals[g0:g0 + n]) + [None] * max(0, g0 + n - len(vals))
        for (i, j) in _SORT16:
            _exchange(g, i, j)
        groups.append(g)
    while len(groups) > 1:
        merged = []
        for k in range(0, len(groups) - 1, 2):
            a, b = groups[k], groups[k + 1]
            m = []
            for i in range(n):
                x, y = a[i], b[n - 1 - i]
                m.append(x if y is None else y if x is None else jnp.maximum(x, y))
            _bitonic_descending(m)
            merged.append(m)
        if len(groups) % 2:
            merged.append(groups[-1])
        groups = merged
    return groups[0]


def _peer_select_kernel(x_ref, g_ref, wq_ref, sk_ref, h2t_ref, c1_ref, e1_ref, r2_ref, e2_ref,
                        q_ref, sc_ref, top_ref, cand_ref, tau_ref, invz_ref):
    tm = x_ref.shape[0]
    h2f = _rms(x_ref[...], g_ref[...])
    h2 = h2f.astype(BF16)
    h2t_ref[...] = h2f.T.astype(BF16)
    q_ref[...] = jnp.dot(h2, wq_ref[...], preferred_element_type=F32).astype(BF16)
    for hp in range(2 * PEER_HEADS):
        sc_ref[hp] = lax.dot_general(sk_ref[hp], q_ref[:, hp * PEER_HALF:(hp + 1) * PEER_HALF],
                                     (((1,), (1,)), ((), ())), preferred_element_type=F32)

    def sort_head(h, _):
        for p in range(2):
            top = _top16_sorted(sc_ref[2 * h + p])
            for a in range(PEER_TOPK):
                top_ref[p, a, pl.ds(h, 1), :] = top[a][0:1, :]
        return 0

    lax.fori_loop(0, PEER_HEADS, sort_head, 0)

    for n, (a, b) in enumerate(_CAND):
        cand_ref[n] = top_ref[0, a] + top_ref[1, b]

    lanes = 2 * LANES
    for c0 in range(0, tm, lanes):
        top = _largest16([cand_ref[n, :, c0:c0 + lanes] for n in range(len(_CAND))])
        tau_ref[:, c0:c0 + lanes] = top[PEER_TOPK - 1]
    tau = tau_ref[...]
    top_sum = cand_ref[0]
    z = jnp.zeros((8, tm), F32)
    for n in range(len(_CAND)):
        ci = cand_ref[n]
        z = z + jnp.where(ci >= tau, jnp.exp(ci - top_sum), 0.0)
    invz_ref[...] = 0.5 / z

    def emit(h, _):
        row = pl.ds(h, 1)
        s1, s2 = sc_ref[2 * h], sc_ref[2 * h + 1]
        tau_h = tau_ref[row, :]
        cnt = jnp.zeros(s1.shape, F32)
        rank = jnp.zeros(s2.shape, F32)
        for b in range(PEER_TOPK):
            s2b = top_ref[1, b, row, :]
            cnt = jnp.where((s1 + s2b) >= tau_h, float(b + 1), cnt)
            rank = jnp.where(s2b > s2, float(b + 1), rank)
        in_top = s1 >= top_ref[0, PEER_TOPK - 1, row, :]
        c1_ref[h] = jnp.where(in_top, cnt, 0.0)
        e1_ref[h] = jnp.exp(s1 - top_ref[0, 0, row, :]) * invz_ref[row, :]
        r2_ref[h] = rank.astype(BF16)
        e2_ref[h] = jnp.exp(s2 - top_ref[1, 0, row, :]).astype(BF16)
        return 0

    lax.fori_loop(0, PEER_HEADS, emit, 0)


def _peer_select(x1, g_ffn, w_q, subkeys, tm=512):
    t, d = x1.shape
    halves = 2 * PEER_HEADS
    sk3 = subkeys.reshape(halves, PEER_KEYS, PEER_HALF)
    shape = (PEER_HEADS, PEER_KEYS, t)
    big_spec = pl.BlockSpec((PEER_HEADS, PEER_KEYS, tm), lambda i: (0, 0, i))
    return pl.pallas_call(
        _peer_select_kernel,
        grid=(t // tm,),
        in_specs=[pl.BlockSpec((tm, d), lambda i: (i, 0)),
                  pl.BlockSpec((1, d), lambda i: (0, 0)),
                  pl.BlockSpec(w_q.shape, lambda i: (0, 0)),
                  pl.BlockSpec(sk3.shape, lambda i: (0, 0, 0))],
        out_specs=[pl.BlockSpec((d, tm), lambda i: (0, i)), big_spec, big_spec, big_spec, big_spec],
        out_shape=[jax.ShapeDtypeStruct((d, t), BF16),
                   jax.ShapeDtypeStruct(shape, F32), jax.ShapeDtypeStruct(shape, F32),
                   jax.ShapeDtypeStruct(shape, BF16), jax.ShapeDtypeStruct(shape, BF16)],
        scratch_shapes=[pltpu.VMEM((tm, halves * PEER_HALF), BF16),
                        pltpu.VMEM((halves, PEER_KEYS, tm), F32),
                        pltpu.VMEM((2, PEER_TOPK, 8, tm), F32),
                        pltpu.VMEM((len(_CAND), 8, tm), F32),
                        pltpu.VMEM((8, tm), F32),
                        pltpu.VMEM((8, tm), F32)],
        compiler_params=_cparams(("parallel",)),
        name="peer_select",
    )(x1, g_ffn.reshape(1, d), w_q, sk3)


PACK = 16
DENSE_ROWS = 8


def _peer_dense_kernel(x_ref, h2t_ref, u_ref, vt_ref, c1_ref, e1_ref, r2_in_ref, e2_in_ref,
                       out_ref, acc_ref, w_ref, act_ref, r2_ref, e2_ref, *, rows, chunk):
    j = pl.program_id(1)
    tm = h2t_ref.shape[1]

    @pl.when(j == 0)
    def _():
        acc_ref[...] = jnp.zeros(acc_ref.shape, F32)
        r2_ref[...] = r2_in_ref[...]
        e2_ref[...] = e2_in_ref[...]

    act_ref[...] = jnp.dot(u_ref[...], h2t_ref[...], preferred_element_type=F32)
    sqrt_half = math.sqrt(0.5)
    n_pack = PEER_KEYS // PACK
    for c in range(tm // chunk):
        tok = slice(c * chunk, (c + 1) * chunk)
        for r0 in range(0, rows, 2):
            gate = [[None] * n_pack for _ in range(2)]
            for h in range(PEER_HEADS):
                c1g = c1_ref[h, :, tok]
                e1g = e1_ref[h, :, tok]
                c1 = [jnp.broadcast_to(c1g[r0 + k:r0 + k + 1, :], (PACK, chunk)).astype(BF16) for k in range(2)]
                e1 = [jnp.broadcast_to(e1g[r0 + k:r0 + k + 1, :], (PACK, chunk)).astype(BF16) for k in range(2)]
                for m in range(n_pack):
                    r2 = r2_ref[h, m * PACK:(m + 1) * PACK, tok]
                    e2 = e2_ref[h, m * PACK:(m + 1) * PACK, tok]
                    for k in range(2):
                        term = jnp.where(r2 < c1[k], e2 * e1[k], jnp.zeros_like(e2))
                        gate[k][m] = term if h == 0 else gate[k][m] + term
            for k in range(2):
                base = (r0 + k) * PEER_KEYS
                for m in range(n_pack):
                    a = act_ref[base + m * PACK:base + (m + 1) * PACK, tok]
                    gelu2 = a * (1.0 + lax.erf(a * sqrt_half))
                    w_ref[base + m * PACK:base + (m + 1) * PACK, tok] = gate[k][m] * gelu2.astype(BF16)
    acc_ref[...] += jnp.dot(vt_ref[...], w_ref[...], preferred_element_type=F32)

    @pl.when(j == pl.num_programs(1) - 1)
    def _():
        out_ref[...] = x_ref[...] + acc_ref[...].T


def _peer_dense(x1, h2t, u, vt, c1, e1, r2, e2, tm=1024, rows=DENSE_ROWS, chunk=128):
    t, d = x1.shape
    n_exp = u.shape[0]
    tn = rows * PEER_KEYS
    big_spec = pl.BlockSpec((PEER_HEADS, PEER_KEYS, tm), lambda i, j: (0, 0, i))
    row_spec = pl.BlockSpec((PEER_HEADS, rows, tm), lambda i, j: (0, j, i))
    return pl.pallas_call(
        functools.partial(_peer_dense_kernel, rows=rows, chunk=chunk),
        grid=(t // tm, n_exp // tn),
        in_specs=[pl.BlockSpec((tm, d), lambda i, j: (i, 0)),
                  pl.BlockSpec((d, tm), lambda i, j: (0, i)),
                  pl.BlockSpec((tn, d), lambda i, j: (j, 0)),
                  pl.BlockSpec((None, d, tn), lambda i, j: (j, 0, 0)),
                  row_spec, row_spec, big_spec, big_spec],
        out_specs=pl.BlockSpec((tm, d), lambda i, j: (i, 0)),
        out_shape=jax.ShapeDtypeStruct((t, d), F32),
        scratch_shapes=[pltpu.VMEM((d, tm), F32), pltpu.VMEM((tn, tm), BF16), pltpu.VMEM((tn, tm), F32),
                        pltpu.VMEM((PEER_HEADS, PEER_KEYS, tm), BF16),
                        pltpu.VMEM((PEER_HEADS, PEER_KEYS, tm), BF16)],
        compiler_params=_cparams(("parallel", "arbitrary")),
        name="peer_dense",
    )(x1, h2t, u, vt, c1, e1, r2, e2)


def kernel(x, mem, g_mix, g_mem, w_in, w_mem_kv, g_q_dil, g_k_dil, g_q_mem, g_k_mem, w_o_sb, w_o_dil,
           w_o_mem, w_gate, b_gate, w_out, g_ffn, w_peer_q, peer_subkeys, peer_u, peer_v):
    batch, seq, d = x.shape
    mem_len = mem.shape[1]
    depth = w_in.shape[0]
    xt = x.reshape(batch * seq, d)
    for l in range(depth):
        w_in_l = w_in[l].astype(BF16)
        c_sb, c_dil = 3 * SB_WIDTH, 3 * SB_WIDTH + 3 * DIL_WIDTH
        sb, dil, mq = _in_proj(xt, g_mix[l], w_in_l, (c_sb, c_dil - c_sb, MEM_WIDTH), (BF16, F32, F32))
        kv = _norm_matmul(mem.reshape(batch * mem_len, d), g_mem[l], w_mem_kv[l].astype(BF16), F32, 512, 512)

        y_sb = _sb_attention(sb.reshape(batch, seq, c_sb), batch, seq)

        g_q2 = jnp.tile(g_q_dil[l], 2).reshape(1, LANES)
        g_k2 = jnp.tile(g_k_dil[l], 2).reshape(1, LANES)
        dil3 = dil.reshape(batch, seq, 3 * DIL_WIDTH)
        o_list, l_list = [], []
        for group in range(len(DIL_CONFIG)):
            o, lse = _dil_group(dil3, g_q2, g_k2, group, batch, seq)
            o_list.append(o.reshape(batch * seq, DIL_OUT_WIDTH))
            l_list.append(lse.reshape(batch * seq, DIL_OUT_WIDTH))

        y_mem = _mem_attention(mq.reshape(batch, seq, MEM_WIDTH), kv.reshape(batch, mem_len, 2 * MEM_WIDTH),
                               g_q_mem[l], g_k_mem[l], batch, seq, mem_len)

        x1 = _merge(xt, g_mix[l], y_sb.reshape(batch * seq, SB_WIDTH), o_list, l_list,
                    y_mem.reshape(batch * seq, MEM_WIDTH), w_gate[l].astype(BF16), b_gate[l],
                    w_o_sb[l].astype(BF16), w_o_dil[l].astype(BF16), w_o_mem[l].astype(BF16),
                    w_out[l].astype(BF16))

        h2t, c1, e1, r2, e2 = _peer_select(x1, g_ffn[l], w_peer_q[l].astype(BF16),
                                          peer_subkeys[l].astype(BF16))
        n_exp = peer_v.shape[1]
        tn = DENSE_ROWS * PEER_KEYS
        vt = peer_v[l].astype(BF16).reshape(n_exp // tn, tn, d).transpose(0, 2, 1)
        xt = _peer_dense(x1, h2t, peer_u[l].astype(BF16), vt, c1, e1, r2, e2)
    return xt.reshape(batch, seq, d)
```

```python
import functools
import math

import numpy as np
import jax
import jax.numpy as jnp
from jax import lax
from jax.experimental import pallas as pl
from jax.experimental.pallas import tpu as pltpu

F32 = jnp.float32
BF16 = jnp.bfloat16

EPS = 1e-6
HEAD_DIM = 64
LANES = 128
SB_HEADS = 8
SB_WIDTH = SB_HEADS * HEAD_DIM
DIL_CONFIG = ((128, 1), (512, 4), (2048, 16))
DIL_HEADS_PER_GROUP = 4
DIL_HEADS = len(DIL_CONFIG) * DIL_HEADS_PER_GROUP
DIL_WIDTH = DIL_HEADS * HEAD_DIM
DIL_OUT_WIDTH = DIL_HEADS_PER_GROUP * HEAD_DIM
DIL_BLOCK = 128
ALIBI_MAX_BIAS = 8.0
MEM_HEADS = 4
MEM_HEAD_DIM = 128
MEM_WIDTH = MEM_HEADS * MEM_HEAD_DIM
PEER_HEADS = 8
PEER_KEYS = 128
PEER_TOPK = 16
PEER_HALF = 128
NEG_BIG = -1e30
LOG2E = 1.4426950408889634
VMEM_LIMIT = 56 * 1024 * 1024


def _cparams(sem, flags=None):
    return pltpu.CompilerParams(dimension_semantics=sem, vmem_limit_bytes=VMEM_LIMIT, flags=flags)


def _rms(x, g):
    ms = jnp.mean(x * x, axis=-1, keepdims=True)
    return x * lax.rsqrt(ms + EPS) * g


def _norm_matmul_kernel(x_ref, g_ref, w_ref, o_ref, h_ref):
    @pl.when(pl.program_id(1) == 0)
    def _():
        h_ref[...] = _rms(x_ref[...], g_ref[...]).astype(BF16)

    o_ref[...] = jnp.dot(h_ref[...], w_ref[...], preferred_element_type=F32).astype(o_ref.dtype)


def _norm_matmul(x2d, g, w, out_dtype, tm, tn):
    m, k = x2d.shape
    n = w.shape[1]
    return pl.pallas_call(
        _norm_matmul_kernel,
        grid=(m // tm, n // tn),
        in_specs=[pl.BlockSpec((tm, k), lambda i, j: (i, 0)),
                  pl.BlockSpec((1, k), lambda i, j: (0, 0)),
                  pl.BlockSpec((k, tn), lambda i, j: (0, j))],
        out_specs=pl.BlockSpec((tm, tn), lambda i, j: (i, j)),
        out_shape=jax.ShapeDtypeStruct((m, n), out_dtype),
        scratch_shapes=[pltpu.VMEM((tm, k), BF16)],
        compiler_params=_cparams(("parallel", "arbitrary")),
        name="norm_matmul",
    )(x2d, g.reshape(1, k), w)


def _in_proj_kernel(x_ref, g_ref, w_ref, *out_refs, tn):
    h = _rms(x_ref[...], g_ref[...]).astype(BF16)
    col = 0
    for o_ref in out_refs:
        width = o_ref.shape[1]
        for c0 in range(0, width, tn):
            c1 = min(c0 + tn, width)
            o_ref[:, c0:c1] = jnp.dot(h, w_ref[:, col + c0:col + c1],
                                      preferred_element_type=F32).astype(o_ref.dtype)
        col += width


def _in_proj(x2d, g, w, widths, dtypes, tm=512, tn=512):
    m, k = x2d.shape
    return pl.pallas_call(
        functools.partial(_in_proj_kernel, tn=tn),
        grid=(m // tm,),
        in_specs=[pl.BlockSpec((tm, k), lambda i: (i, 0)),
                  pl.BlockSpec((1, k), lambda i: (0, 0)),
                  pl.BlockSpec(w.shape, lambda i: (0, 0))],
        out_specs=[pl.BlockSpec((tm, n), lambda i: (i, 0)) for n in widths],
        out_shape=[jax.ShapeDtypeStruct((m, n), dt) for n, dt in zip(widths, dtypes)],
        compiler_params=_cparams(("parallel",)),
        name="in_proj",
    )(x2d, g.reshape(1, k), w)


SB_GROUP = 4


def _sb_kernel(q_ref, k_ref, v_ref, o_ref, acc_ref, *, tq):
    qi = pl.program_id(2)
    width = SB_GROUP * HEAD_DIM
    head_of_lane = lax.broadcasted_iota(jnp.int32, (tq, width), 1) // HEAD_DIM
    row = lax.broadcasted_iota(jnp.int32, (tq, tq), 0)
    col = lax.broadcasted_iota(jnp.int32, (tq, tq), 1)
    neg_later = -((row > col).astype(BF16))
    past = col < row
    q_all = q_ref[...] * (HEAD_DIM ** -0.5)
    zero = jnp.zeros_like(q_all)
    qm = [jnp.where(head_of_lane == h, q_all, zero) for h in range(SB_GROUP)]

    def block(j, newer, diagonal):
        start = pl.multiple_of(j * tq, tq)
        kb = k_ref[pl.ds(start, tq), :]
        vb = v_ref[pl.ds(start, tq), :]
        weights, out = [], []
        for h in range(SB_GROUP):
            z2 = lax.dot_general(qm[h], kb, (((1,), (1,)), ((), ())), preferred_element_type=F32) * LOG2E
            sp = jnp.maximum(z2, 0.0) + jnp.log2(1.0 + jnp.exp2(-jnp.abs(z2)))
            if diagonal:
                sp = jnp.where(past, sp, 0.0)
            local = jnp.dot(sp.astype(BF16), neg_later, preferred_element_type=F32)
            a = jnp.exp2((z2 - sp) + (local + newer[h]))
            if diagonal:
                a = jnp.where(past, a, 0.0)
            weights.append(a.astype(BF16))
            out.append(newer[h] - jnp.sum(sp, axis=-1, keepdims=True))
        v_stack = jnp.concatenate([jnp.where(head_of_lane == h, vb, jnp.zeros_like(vb)) for h in range(SB_GROUP)],
                                  axis=0)
        pv = jnp.dot(jnp.concatenate(weights, axis=1), v_stack, preferred_element_type=F32)
        if diagonal:
            acc_ref[...] = pv
        else:
            acc_ref[...] += pv
        return tuple(out)

    zeros = jnp.zeros((tq, 1), F32)
    odd = qi % 2
    start = (zeros,) * SB_GROUP
    newer = lax.cond(odd == 1,
                     lambda: block(qi - 1, block(qi, start, True), False),
                     lambda: block(qi, start, True))

    def two_blocks(p, nw):
        j = qi - 1 - odd - 2 * p
        return block(j - 1, block(j, nw, False), False)

    lax.fori_loop(0, qi // 2, two_blocks, newer)
    o_ref[...] = acc_ref[...].astype(o_ref.dtype)


def _sb_attention(sb, batch, seq, tq=256):
    width = SB_GROUP * HEAD_DIM
    groups = SB_WIDTH // width
    return pl.pallas_call(
        functools.partial(_sb_kernel, tq=tq),
        grid=(batch, groups, seq // tq),
        in_specs=[pl.BlockSpec((None, tq, width), lambda b, p, i: (b, i, p)),
                  pl.BlockSpec((None, seq, width), lambda b, p, i: (b, 0, groups + p)),
                  pl.BlockSpec((None, seq, width), lambda b, p, i: (b, 0, 2 * groups + p))],
        out_specs=pl.BlockSpec((None, tq, width), lambda b, p, i: (b, i, p)),
        out_shape=jax.ShapeDtypeStruct((batch, seq, SB_WIDTH), BF16),
        scratch_shapes=[pltpu.VMEM((tq, width), F32)],
        compiler_params=_cparams(("parallel", "parallel", "parallel")),
        name="sb_attention",
    )(sb, sb, sb)


def _head_rms(x, g, lane):
    xx = x * x
    lo = lane < HEAD_DIM
    s_lo = jnp.sum(jnp.where(lo, xx, 0.0), axis=-1, keepdims=True)
    s_hi = jnp.sum(jnp.where(lo, 0.0, xx), axis=-1, keepdims=True)
    ms = jnp.where(lo, s_lo, s_hi) * (1.0 / HEAD_DIM)
    return x * lax.rsqrt(ms + EPS) * g


def _dil_kernel(q_ref, k_ref, v_ref, gq_ref, gk_ref, o_ref, l_ref, qn_ref, kn_ref, vn_ref,
                *, dilation, length, slopes, scale, unroll):
    nb = length // DIL_BLOCK
    pair = pl.program_id(1)
    lane_l = lax.broadcasted_iota(jnp.int32, (length, LANES), 1)
    lane_b = lax.broadcasted_iota(jnp.int32, (DIL_BLOCK, LANES), 1)
    qi = lax.broadcasted_iota(jnp.int32, (DIL_BLOCK, 2 * DIL_BLOCK), 0)
    kj = lax.broadcasted_iota(jnp.int32, (DIL_BLOCK, 2 * DIL_BLOCK), 1) - DIL_BLOCK
    gap = qi - kj
    band = (gap >= 0) & (gap <= DIL_BLOCK)
    gapf = (gap * dilation).astype(F32)
    bias_rest, bias_first = [], []
    for half in range(2):
        slope = jnp.where(pair == 0, slopes[half], slopes[2 + half])
        bias_rest.append(jnp.where(band, -(slope * gapf), NEG_BIG))
        bias_first.append(jnp.where(band & (kj >= 0), -(slope * gapf), NEG_BIG))

    zero_block = jnp.zeros((DIL_BLOCK, LANES), BF16)
    for c in range(dilation):
        rows = pl.ds(0, length) if dilation == 1 else pl.ds(c, length, stride=dilation)
        qn_ref[c] = _head_rms(q_ref[rows, :], gq_ref[...], lane_l).astype(BF16) * scale
        kn_ref[c, pl.ds(0, DIL_BLOCK), :] = zero_block
        vn_ref[c, pl.ds(0, DIL_BLOCK), :] = zero_block
        kn_ref[c, pl.ds(DIL_BLOCK, length), :] = _head_rms(k_ref[rows, :], gk_ref[...], lane_l).astype(BF16)
        vn_ref[c, pl.ds(DIL_BLOCK, length), :] = v_ref[rows, :].astype(BF16)

    def block(b):
        c, n = b // nb, b % nb
        start = pl.multiple_of(n * DIL_BLOCK, DIL_BLOCK)
        qb = qn_ref[c, pl.ds(start, DIL_BLOCK), :]
        kw = kn_ref[c, pl.ds(start, 2 * DIL_BLOCK), :]
        vw = vn_ref[c, pl.ds(start, 2 * DIL_BLOCK), :]
        outs, lses = [], []
        for half in range(2):
            in_head = (lane_b < HEAD_DIM) if half == 0 else (lane_b >= HEAD_DIM)
            qm = jnp.where(in_head, qb, jnp.zeros_like(qb))
            s = lax.dot_general(qm, kw, (((1,), (1,)), ((), ())), preferred_element_type=F32)
            s = s + jnp.where(n == 0, bias_first[half], bias_rest[half])
            m = jnp.max(s, axis=-1, keepdims=True)
            e = jnp.exp(s - m)
            denom = jnp.sum(e, axis=-1, keepdims=True)
            o = jnp.dot(e.astype(BF16), vw, preferred_element_type=F32) / denom
            outs.append(o)
            lses.append(m + jnp.log(denom))
        lo = lane_b < HEAD_DIM
        if dilation == 1:
            dst = pl.ds(start, DIL_BLOCK)
        else:
            dst = pl.ds(c + start * dilation, DIL_BLOCK, stride=dilation)
        o_ref[dst, :] = jnp.where(lo, outs[0], outs[1])
        l_ref[dst, :] = jnp.where(lo, lses[0], lses[1])

    def blocks(g, _):
        for k in range(unroll):
            block(g * unroll + k)
        return 0

    lax.fori_loop(0, dilation * nb // unroll, blocks, 0)


def _dil_group(dil, g_q2, g_k2, group, batch, seq):
    _, dilation = DIL_CONFIG[group]
    length = seq // dilation
    blocks = DIL_WIDTH // LANES
    pairs = DIL_OUT_WIDTH // LANES
    all_slopes = [2.0 ** (-ALIBI_MAX_BIAS * (i + 1) / DIL_HEADS) for i in range(DIL_HEADS)]
    slopes = tuple(float(np.float32(s)) for s in
                   all_slopes[group * DIL_HEADS_PER_GROUP:(group + 1) * DIL_HEADS_PER_GROUP])
    kern = functools.partial(_dil_kernel, dilation=dilation, length=length, slopes=slopes,
                             scale=HEAD_DIM ** -0.5, unroll=8)
    col = lambda off: (lambda b, p: (b, 0, off + group * pairs + p))
    out_sd = jax.ShapeDtypeStruct((batch, seq, DIL_OUT_WIDTH), F32)
    return pl.pallas_call(
        kern,
        grid=(batch, pairs),
        in_specs=[pl.BlockSpec((None, seq, LANES), col(0)),
                  pl.BlockSpec((None, seq, LANES), col(blocks)),
                  pl.BlockSpec((None, seq, LANES), col(2 * blocks)),
                  pl.BlockSpec((1, LANES), lambda b, p: (0, 0)),
                  pl.BlockSpec((1, LANES), lambda b, p: (0, 0))],
        out_specs=[pl.BlockSpec((None, seq, LANES), lambda b, p: (b, 0, p)),
                   pl.BlockSpec((None, seq, LANES), lambda b, p: (b, 0, p))],
        out_shape=[out_sd, out_sd],
        scratch_shapes=[pltpu.VMEM((dilation, length, LANES), BF16),
                        pltpu.VMEM((dilation, length + DIL_BLOCK, LANES), BF16),
                        pltpu.VMEM((dilation, length + DIL_BLOCK, LANES), BF16)],
        compiler_params=_cparams(("parallel", "parallel")),
        name=f"dilated_attention_r{dilation}",
    )(dil, dil, dil, g_q2, g_k2)


def _mem_kernel(q_ref, kv_ref, gq_ref, gk_ref, o_ref, *, scale):
    for h in range(MEM_HEADS):
        cols = slice(h * MEM_HEAD_DIM, (h + 1) * MEM_HEAD_DIM)
        vcols = slice(MEM_WIDTH + h * MEM_HEAD_DIM, MEM_WIDTH + (h + 1) * MEM_HEAD_DIM)
        q = _rms(q_ref[:, cols], gq_ref[...]).astype(BF16)
        k = _rms(kv_ref[:, cols], gk_ref[...]).astype(BF16)
        s = lax.dot_general(q, k, (((1,), (1,)), ((), ())), preferred_element_type=F32) * scale
        m = jnp.max(s, axis=-1, keepdims=True)
        e = jnp.exp(s - m)
        denom = jnp.sum(e, axis=-1, keepdims=True)
        o = jnp.dot(e.astype(BF16), kv_ref[:, vcols].astype(BF16), preferred_element_type=F32) / denom
        o_ref[:, cols] = o.astype(o_ref.dtype)


def _mem_attention(mq, kv, g_q, g_k, batch, seq, mem_len, tq=512):
    return pl.pallas_call(
        functools.partial(_mem_kernel, scale=MEM_HEAD_DIM ** -0.5),
        grid=(batch, seq // tq),
        in_specs=[pl.BlockSpec((None, tq, MEM_WIDTH), lambda b, i: (b, i, 0)),
                  pl.BlockSpec((None, mem_len, 2 * MEM_WIDTH), lambda b, i: (b, 0, 0)),
                  pl.BlockSpec((1, LANES), lambda b, i: (0, 0)),
                  pl.BlockSpec((1, LANES), lambda b, i: (0, 0))],
        out_specs=pl.BlockSpec((None, tq, MEM_WIDTH), lambda b, i: (b, i, 0)),
        out_shape=jax.ShapeDtypeStruct((batch, seq, MEM_WIDTH), BF16),
        compiler_params=_cparams(("parallel", "parallel")),
        name="memory_attention",
    )(mq, kv, g_q.reshape(1, LANES), g_k.reshape(1, LANES))


def _merge_kernel(x_ref, g_ref, ysb_ref, o0_ref, o1_ref, o2_ref, l0_ref, l1_ref, l2_ref, ymem_ref,
                  wg_ref, bg_ref, wsb_ref, wdil_ref, wmem_ref, wout_ref, out_ref):
    d = x_ref.shape[-1]
    x = x_ref[...]
    h = _rms(x, g_ref[...]).astype(BF16)
    l0, l1, l2 = l0_ref[...], l1_ref[...], l2_ref[...]
    m = jnp.maximum(jnp.maximum(l0, l1), l2)
    e0, e1, e2 = jnp.exp(l0 - m), jnp.exp(l1 - m), jnp.exp(l2 - m)
    inv = 1.0 / (e0 + e1 + e2)
    y_dil = ((e0 * inv) * o0_ref[...] + (e1 * inv) * o1_ref[...] + (e2 * inv) * o2_ref[...]).astype(BF16)
    merged = jnp.zeros(x.shape, F32)
    branches = ((ysb_ref[...], wsb_ref), (y_dil, wdil_ref), (ymem_ref[...], wmem_ref))
    for n, (y, w_ref) in enumerate(branches):
        pre = jnp.dot(h, wg_ref[:, n * d:(n + 1) * d], preferred_element_type=F32) + bg_ref[:, n * d:(n + 1) * d]
        gate = jax.nn.sigmoid(pre)
        merged = merged + gate * jnp.dot(y, w_ref[...], preferred_element_type=F32)
    out_ref[...] = x + jnp.dot(merged.astype(BF16), wout_ref[...], preferred_element_type=F32)


def _merge(x2d, g_mix, y_sb, o_list, l_list, y_mem, w_gate, b_gate, w_o_sb, w_o_dil, w_o_mem, w_out, tm=256):
    t, d = x2d.shape
    row = lambda w: pl.BlockSpec((tm, w), lambda i: (i, 0))
    full = lambda a: pl.BlockSpec(a.shape, lambda i: (0, 0))
    g2 = g_mix.reshape(1, d)
    b2 = b_gate.reshape(1, -1)
    args = [x2d, g2, y_sb, *o_list, *l_list, y_mem, w_gate, b2, w_o_sb, w_o_dil, w_o_mem, w_out]
    specs = [row(d), full(g2), row(SB_WIDTH)] + [row(DIL_OUT_WIDTH)] * 6 + [row(MEM_WIDTH)] + \
            [full(a) for a in (w_gate, b2, w_o_sb, w_o_dil, w_o_mem, w_out)]
    return pl.pallas_call(
        _merge_kernel,
        grid=(t // tm,),
        in_specs=specs,
        out_specs=row(d),
        out_shape=jax.ShapeDtypeStruct((t, d), F32),
        compiler_params=_cparams(("parallel",)),
        name="merge_project",
    )(*args)


def _oddeven_merge_sort_pairs(n):
    pairs = []

    def merge(lo, hi, r):
        step = r * 2
        if step < hi - lo:
            merge(lo, hi, step)
            merge(lo + r, hi, step)
            for i in range(lo + r, hi - r, step):
                pairs.append((i, i + r))
        else:
            pairs.append((lo, lo + r))

    def sort(lo, hi):
        if hi - lo >= 1:
            mid = lo + (hi - lo) // 2
            sort(lo, mid)
            sort(mid + 1, hi)
            merge(lo, hi, 1)

    sort(0, n - 1)
    return pairs


_SORT16 = _oddeven_merge_sort_pairs(PEER_TOPK)
_CAND = [(a, b) for a in range(PEER_TOPK) for b in range(PEER_TOPK) if (a + 1) * (b + 1) <= PEER_TOPK]


def _top16_sorted(sc):
    n = PEER_KEYS // 8
    v = [sc[8 * i:8 * (i + 1), :] for i in range(n)]
    for (i, j) in _SORT16:
        hi, lo = jnp.maximum(v[i], v[j]), jnp.minimum(v[i], v[j])
        v[i], v[j] = hi, lo
    for shift in (4, 2, 1):
        other = [pltpu.roll(x, shift, 0) for x in v]
        v = [jnp.maximum(v[i], other[n - 1 - i]) for i in range(n)]
        d = n // 2
        while d >= 1:
            for i in range(n):
                if (i // d) % 2 == 0:
                    hi, lo = jnp.maximum(v[i], v[i + d]), jnp.minimum(v[i], v[i + d])
                    v[i], v[i + d] = hi, lo
            d //= 2
    return v


def _exchange(v, i, j):
    a, b = v[i], v[j]
    if b is None:
        return
    if a is None:
        v[i], v[j] = b, None
    else:
        v[i], v[j] = jnp.maximum(a, b), jnp.minimum(a, b)


def _bitonic_descending(v):
    n = len(v)
    d = n // 2
    while d >= 1:
        for i in range(n):
            if (i // d) % 2 == 0:
                _exchange(v, i, i + d)
        d //= 2


def _largest16(vals):
    n = PEER_TOPK
    groups = []
    for g0 in range(0, len(vals), n):
        g = list(vals[g0:g0 + n]) + [None] * max(0, g0 + n - len(vals))
        for (i, j) in _SORT16:
            _exchange(g, i, j)
        groups.append(g)
    while len(groups) > 1:
        merged = []
        for k in range(0, len(groups) - 1, 2):
            a, b = groups[k], groups[k + 1]
            m = []
            for i in range(n):
                x, y = a[i], b[n - 1 - i]
                m.append(x if y is None else y if x is None else jnp.maximum(x, y))
            _bitonic_descending(m)
            merged.append(m)
        if len(groups) % 2:
            merged.append(groups[-1])
        groups = merged
    return groups[0]


def _peer_select_kernel(x_ref, g_ref, wq_ref, sk_ref, h2t_ref, c1_ref, e1_ref, r2_ref, e2_ref,
                        q_ref, sc_ref, top_ref, cand_ref, tau_ref, invz_ref):
    tm = x_ref.shape[0]
    h2f = _rms(x_ref[...], g_ref[...])
    h2 = h2f.astype(BF16)
    h2t_ref[...] = h2f.T.astype(BF16)
    q_ref[...] = jnp.dot(h2, wq_ref[...], preferred_element_type=F32).astype(BF16)
    for hp in range(2 * PEER_HEADS):
        sc_ref[hp] = lax.dot_general(sk_ref[hp], q_ref[:, hp * PEER_HALF:(hp + 1) * PEER_HALF],
                                     (((1,), (1,)), ((), ())), preferred_element_type=F32)

    def sort_head(h, _):
        for p in range(2):
            top = _top16_sorted(sc_ref[2 * h + p])
            for a in range(PEER_TOPK):
                top_ref[p, a, pl.ds(h, 1), :] = top[a][0:1, :]
        return 0

    lax.fori_loop(0, PEER_HEADS, sort_head, 0)

    for n, (a, b) in enumerate(_CAND):
        cand_ref[n] = top_ref[0, a] + top_ref[1, b]

    lanes = 2 * LANES
    for c0 in range(0, tm, lanes):
        top = _largest16([cand_ref[n, :, c0:c0 + lanes] for n in range(len(_CAND))])
        tau_ref[:, c0:c0 + lanes] = top[PEER_TOPK - 1]
    tau = tau_ref[...]
    top_sum = cand_ref[0]
    z = jnp.zeros((8, tm), F32)
    for n in range(len(_CAND)):
        ci = cand_ref[n]
        z = z + jnp.where(ci >= tau, jnp.exp(ci - top_sum), 0.0)
    invz_ref[...] = 0.5 / z

    def emit(h, _):
        row = pl.ds(h, 1)
        s1, s2 = sc_ref[2 * h], sc_ref[2 * h + 1]
        tau_h = tau_ref[row, :]
        cnt = jnp.zeros(s1.shape, F32)
        rank = jnp.zeros(s2.shape, F32)
        for b in range(PEER_TOPK):
            s2b = top_ref[1, b, row, :]
            cnt = jnp.where((s1 + s2b) >= tau_h, float(b + 1), cnt)
            rank = jnp.where(s2b > s2, float(b + 1), rank)
        in_top = s1 >= top_ref[0, PEER_TOPK - 1, row, :]
        c1_ref[h] = jnp.where(in_top, cnt, 0.0)
        e1_ref[h] = jnp.exp(s1 - top_ref[0, 0, row, :]) * invz_ref[row, :]
        r2_ref[h] = rank
        e2_ref[h] = jnp.exp(s2 - top_ref[1, 0, row, :])
        return 0

    lax.fori_loop(0, PEER_HEADS, emit, 0)


def _peer_select(x1, g_ffn, w_q, subkeys, tm=512):
    t, d = x1.shape
    halves = 2 * PEER_HEADS
    sk3 = subkeys.reshape(halves, PEER_KEYS, PEER_HALF)
    shape = (PEER_HEADS, PEER_KEYS, t)
    big_spec = pl.BlockSpec((PEER_HEADS, PEER_KEYS, tm), lambda i: (0, 0, i))
    return pl.pallas_call(
        _peer_select_kernel,
        grid=(t // tm,),
        in_specs=[pl.BlockSpec((tm, d), lambda i: (i, 0)),
                  pl.BlockSpec((1, d), lambda i: (0, 0)),
                  pl.BlockSpec(w_q.shape, lambda i: (0, 0)),
                  pl.BlockSpec(sk3.shape, lambda i: (0, 0, 0))],
        out_specs=[pl.BlockSpec((d, tm), lambda i: (0, i)), big_spec, big_spec, big_spec, big_spec],
        out_shape=[jax.ShapeDtypeStruct((d, t), BF16),
                   jax.ShapeDtypeStruct(shape, F32), jax.ShapeDtypeStruct(shape, F32),
                   jax.ShapeDtypeStruct(shape, F32), jax.ShapeDtypeStruct(shape, F32)],
        scratch_shapes=[pltpu.VMEM((tm, halves * PEER_HALF), BF16),
                        pltpu.VMEM((halves, PEER_KEYS, tm), F32),
                        pltpu.VMEM((2, PEER_TOPK, 8, tm), F32),
                        pltpu.VMEM((len(_CAND), 8, tm), F32),
                        pltpu.VMEM((8, tm), F32),
                        pltpu.VMEM((8, tm), F32)],
        compiler_params=_cparams(("parallel",)),
        name="peer_select",
    )(x1, g_ffn.reshape(1, d), w_q, sk3)


SUBLANES = 8
DENSE_ROWS = 8


def _peer_dense_kernel(x_ref, h2t_ref, u_ref, vt_ref, c1_ref, e1_ref, r2_ref, e2_ref,
                       out_ref, acc_ref, w_ref, act_ref, *, rows, chunk):
    j = pl.program_id(1)
    tm = h2t_ref.shape[1]

    @pl.when(j == 0)
    def _():
        acc_ref[...] = jnp.zeros(acc_ref.shape, F32)

    act_ref[...] = jnp.dot(u_ref[...], h2t_ref[...], preferred_element_type=F32)
    sqrt_half = math.sqrt(0.5)
    n_tiles = PEER_KEYS // SUBLANES
    for c in range(tm // chunk):
        tok = slice(c * chunk, (c + 1) * chunk)
        for r0 in range(0, rows, 2):
            gate = [[None] * n_tiles for _ in range(2)]
            for h in range(PEER_HEADS):
                c1g = c1_ref[h, :, tok]
                e1g = e1_ref[h, :, tok]
                c1 = [jnp.broadcast_to(c1g[r0 + k:r0 + k + 1, :], (SUBLANES, chunk)) for k in range(2)]
                e1 = [jnp.broadcast_to(e1g[r0 + k:r0 + k + 1, :], (SUBLANES, chunk)) for k in range(2)]
                for m in range(n_tiles):
                    r2 = r2_ref[h, m * SUBLANES:(m + 1) * SUBLANES, tok]
                    e2 = e2_ref[h, m * SUBLANES:(m + 1) * SUBLANES, tok]
                    for k in range(2):
                        term = jnp.where(r2 < c1[k], e2 * e1[k], 0.0)
                        gate[k][m] = term if h == 0 else gate[k][m] + term
            for k in range(2):
                base = (r0 + k) * PEER_KEYS
                for m in range(0, n_tiles, 2):
                    rows_ = slice(base + m * SUBLANES, base + (m + 2) * SUBLANES)
                    a = act_ref[rows_, tok]
                    gelu2 = a * (1.0 + lax.erf(a * sqrt_half))
                    g = jnp.concatenate([gate[k][m], gate[k][m + 1]], axis=0)
                    w_ref[rows_, tok] = (g * gelu2).astype(BF16)
    acc_ref[...] += jnp.dot(vt_ref[...], w_ref[...], preferred_element_type=F32)

    @pl.when(j == pl.num_programs(1) - 1)
    def _():
        out_ref[...] = x_ref[...] + acc_ref[...].T


def _peer_dense(x1, h2t, u, vt, c1, e1, r2, e2, tm=1024, rows=DENSE_ROWS, chunk=128):
    t, d = x1.shape
    n_exp = u.shape[0]
    tn = rows * PEER_KEYS
    big_spec = pl.BlockSpec((PEER_HEADS, PEER_KEYS, tm), lambda i, j: (0, 0, i))
    row_spec = pl.BlockSpec((PEER_HEADS, rows, tm), lambda i, j: (0, j, i))
    return pl.pallas_call(
        functools.partial(_peer_dense_kernel, rows=rows, chunk=chunk),
        grid=(t // tm, n_exp // tn),
        in_specs=[pl.BlockSpec((tm, d), lambda i, j: (i, 0)),
                  pl.BlockSpec((d, tm), lambda i, j: (0, i)),
                  pl.BlockSpec((tn, d), lambda i, j: (j, 0)),
                  pl.BlockSpec((None, d, tn), lambda i, j: (j, 0, 0)),
                  row_spec, row_spec, big_spec, big_spec],
        out_specs=pl.BlockSpec((tm, d), lambda i, j: (i, 0)),
        out_shape=jax.ShapeDtypeStruct((t, d), F32),
        scratch_shapes=[pltpu.VMEM((d, tm), F32), pltpu.VMEM((tn, tm), BF16), pltpu.VMEM((tn, tm), F32)],
        compiler_params=_cparams(("parallel", "arbitrary")),
        name="peer_dense",
    )(x1, h2t, u, vt, c1, e1, r2, e2)


def kernel(x, mem, g_mix, g_mem, w_in, w_mem_kv, g_q_dil, g_k_dil, g_q_mem, g_k_mem, w_o_sb, w_o_dil,
           w_o_mem, w_gate, b_gate, w_out, g_ffn, w_peer_q, peer_subkeys, peer_u, peer_v):
    batch, seq, d = x.shape
    mem_len = mem.shape[1]
    depth = w_in.shape[0]
    xt = x.reshape(batch * seq, d)
    for l in range(depth):
        w_in_l = w_in[l].astype(BF16)
        c_sb, c_dil = 3 * SB_WIDTH, 3 * SB_WIDTH + 3 * DIL_WIDTH
        sb, dil, mq = _in_proj(xt, g_mix[l], w_in_l, (c_sb, c_dil - c_sb, MEM_WIDTH), (BF16, F32, F32))
        kv = _norm_matmul(mem.reshape(batch * mem_len, d), g_mem[l], w_mem_kv[l].astype(BF16), F32, 512, 512)

        y_sb = _sb_attention(sb.reshape(batch, seq, c_sb), batch, seq)

        g_q2 = jnp.tile(g_q_dil[l], 2).reshape(1, LANES)
        g_k2 = jnp.tile(g_k_dil[l], 2).reshape(1, LANES)
        dil3 = dil.reshape(batch, seq, 3 * DIL_WIDTH)
        o_list, l_list = [], []
        for group in range(len(DIL_CONFIG)):
            o, lse = _dil_group(dil3, g_q2, g_k2, group, batch, seq)
            o_list.append(o.reshape(batch * seq, DIL_OUT_WIDTH))
            l_list.append(lse.reshape(batch * seq, DIL_OUT_WIDTH))

        y_mem = _mem_attention(mq.reshape(batch, seq, MEM_WIDTH), kv.reshape(batch, mem_len, 2 * MEM_WIDTH),
                               g_q_mem[l], g_k_mem[l], batch, seq, mem_len)

        x1 = _merge(xt, g_mix[l], y_sb.reshape(batch * seq, SB_WIDTH), o_list, l_list,
                    y_mem.reshape(batch * seq, MEM_WIDTH), w_gate[l].astype(BF16), b_gate[l],
                    w_o_sb[l].astype(BF16), w_o_dil[l].astype(BF16), w_o_mem[l].astype(BF16),
                    w_out[l].astype(BF16))

        h2t, c1, e1, r2, e2 = _peer_select(x1, g_ffn[l], w_peer_q[l].astype(BF16),
                                          peer_subkeys[l].astype(BF16))
        n_exp = peer_v.shape[1]
        tn = DENSE_ROWS * PEER_KEYS
        vt = peer_v[l].astype(BF16).reshape(n_exp // tn, tn, d).transpose(0, 2, 1)
        xt = _peer_dense(x1, h2t, peer_u[l].astype(BF16), vt, c1, e1, r2, e2)
    return xt.reshape(batch, seq, d)
```

```python
import functools
import math

import numpy as np
import jax
import jax.numpy as jnp
from jax import lax
from jax.experimental import pallas as pl
from jax.experimental.pallas import tpu as pltpu

F32 = jnp.float32
BF16 = jnp.bfloat16

EPS = 1e-6
HEAD_DIM = 64
LANES = 128
SB_HEADS = 8
SB_WIDTH = SB_HEADS * HEAD_DIM
DIL_CONFIG = ((128, 1), (512, 4), (2048, 16))
DIL_HEADS_PER_GROUP = 4
DIL_HEADS = len(DIL_CONFIG) * DIL_HEADS_PER_GROUP
DIL_WIDTH = DIL_HEADS * HEAD_DIM
DIL_OUT_WIDTH = DIL_HEADS_PER_GROUP * HEAD_DIM
DIL_BLOCK = 128
ALIBI_MAX_BIAS = 8.0
MEM_HEADS = 4
MEM_HEAD_DIM = 128
MEM_WIDTH = MEM_HEADS * MEM_HEAD_DIM
PEER_HEADS = 8
PEER_KEYS = 128
PEER_TOPK = 16
PEER_HALF = 128
NEG_BIG = -1e30
LOG2E = 1.4426950408889634
VMEM_LIMIT = 56 * 1024 * 1024


def _cparams(sem, flags=None):
    return pltpu.CompilerParams(dimension_semantics=sem, vmem_limit_bytes=VMEM_LIMIT, flags=flags)


def _rms(x, g):
    ms = jnp.mean(x * x, axis=-1, keepdims=True)
    return x * lax.rsqrt(ms + EPS) * g


def _norm_matmul_kernel(x_ref, g_ref, w_ref, o_ref, h_ref):
    @pl.when(pl.program_id(1) == 0)
    def _():
        h_ref[...] = _rms(x_ref[...], g_ref[...]).astype(BF16)

    o_ref[...] = jnp.dot(h_ref[...], w_ref[...], preferred_element_type=F32).astype(o_ref.dtype)


def _norm_matmul(x2d, g, w, out_dtype, tm, tn):
    m, k = x2d.shape
    n = w.shape[1]
    return pl.pallas_call(
        _norm_matmul_kernel,
        grid=(m // tm, n // tn),
        in_specs=[pl.BlockSpec((tm, k), lambda i, j: (i, 0)),
                  pl.BlockSpec((1, k), lambda i, j: (0, 0)),
                  pl.BlockSpec((k, tn), lambda i, j: (0, j))],
        out_specs=pl.BlockSpec((tm, tn), lambda i, j: (i, j)),
        out_shape=jax.ShapeDtypeStruct((m, n), out_dtype),
        scratch_shapes=[pltpu.VMEM((tm, k), BF16)],
        compiler_params=_cparams(("parallel", "arbitrary")),
        name="norm_matmul",
    )(x2d, g.reshape(1, k), w)


def _in_proj_kernel(x_ref, g_ref, w_ref, *out_refs, tn):
    h = _rms(x_ref[...], g_ref[...]).astype(BF16)
    col = 0
    for o_ref in out_refs:
        width = o_ref.shape[1]
        for c0 in range(0, width, tn):
            c1 = min(c0 + tn, width)
            o_ref[:, c0:c1] = jnp.dot(h, w_ref[:, col + c0:col + c1],
                                      preferred_element_type=F32).astype(o_ref.dtype)
        col += width


def _in_proj(x2d, g, w, widths, dtypes, tm=512, tn=512):
    m, k = x2d.shape
    return pl.pallas_call(
        functools.partial(_in_proj_kernel, tn=tn),
        grid=(m // tm,),
        in_specs=[pl.BlockSpec((tm, k), lambda i: (i, 0)),
                  pl.BlockSpec((1, k), lambda i: (0, 0)),
                  pl.BlockSpec(w.shape, lambda i: (0, 0))],
        out_specs=[pl.BlockSpec((tm, n), lambda i: (i, 0)) for n in widths],
        out_shape=[jax.ShapeDtypeStruct((m, n), dt) for n, dt in zip(widths, dtypes)],
        compiler_params=_cparams(("parallel",)),
        name="in_proj",
    )(x2d, g.reshape(1, k), w)


SB_GROUP = 4


def _sb_kernel(q_ref, k_ref, v_ref, o_ref, acc_ref, *, tq):
    qi = pl.program_id(2)
    width = SB_GROUP * HEAD_DIM
    head_of_lane = lax.broadcasted_iota(jnp.int32, (tq, width), 1) // HEAD_DIM
    row = lax.broadcasted_iota(jnp.int32, (tq, tq), 0)
    col = lax.broadcasted_iota(jnp.int32, (tq, tq), 1)
    neg_later = -((row > col).astype(BF16))
    past = col < row
    q_all = q_ref[...] * (HEAD_DIM ** -0.5)
    zero = jnp.zeros_like(q_all)
    qm = [jnp.where(head_of_lane == h, q_all, zero) for h in range(SB_GROUP)]

    def block(j, newer, diagonal):
        start = pl.multiple_of(j * tq, tq)
        kb = k_ref[pl.ds(start, tq), :]
        vb = v_ref[pl.ds(start, tq), :]
        weights, out = [], []
        for h in range(SB_GROUP):
            z2 = lax.dot_general(qm[h], kb, (((1,), (1,)), ((), ())), preferred_element_type=F32) * LOG2E
            sp = jnp.maximum(z2, 0.0) + jnp.log2(1.0 + jnp.exp2(-jnp.abs(z2)))
            if diagonal:
                sp = jnp.where(past, sp, 0.0)
            local = jnp.dot(sp.astype(BF16), neg_later, preferred_element_type=F32)
            a = jnp.exp2((z2 - sp) + (local + newer[h]))
            if diagonal:
                a = jnp.where(past, a, 0.0)
            weights.append(a.astype(BF16))
            out.append(newer[h] - jnp.sum(sp, axis=-1, keepdims=True))
        v_stack = jnp.concatenate([jnp.where(head_of_lane == h, vb, jnp.zeros_like(vb)) for h in range(SB_GROUP)],
                                  axis=0)
        pv = jnp.dot(jnp.concatenate(weights, axis=1), v_stack, preferred_element_type=F32)
        if diagonal:
            acc_ref[...] = pv
        else:
            acc_ref[...] += pv
        return tuple(out)

    zeros = jnp.zeros((tq, 1), F32)
    odd = qi % 2
    start = (zeros,) * SB_GROUP
    newer = lax.cond(odd == 1,
                     lambda: block(qi - 1, block(qi, start, True), False),
                     lambda: block(qi, start, True))

    def two_blocks(p, nw):
        j = qi - 1 - odd - 2 * p
        return block(j - 1, block(j, nw, False), False)

    lax.fori_loop(0, qi // 2, two_blocks, newer)
    o_ref[...] = acc_ref[...].astype(o_ref.dtype)


def _sb_attention(sb, batch, seq, tq=256):
    width = SB_GROUP * HEAD_DIM
    groups = SB_WIDTH // width
    return pl.pallas_call(
        functools.partial(_sb_kernel, tq=tq),
        grid=(batch, groups, seq // tq),
        in_specs=[pl.BlockSpec((None, tq, width), lambda b, p, i: (b, i, p)),
                  pl.BlockSpec((None, seq, width), lambda b, p, i: (b, 0, groups + p)),
                  pl.BlockSpec((None, seq, width), lambda b, p, i: (b, 0, 2 * groups + p))],
        out_specs=pl.BlockSpec((None, tq, width), lambda b, p, i: (b, i, p)),
        out_shape=jax.ShapeDtypeStruct((batch, seq, SB_WIDTH), BF16),
        scratch_shapes=[pltpu.VMEM((tq, width), F32)],
        compiler_params=_cparams(("parallel", "parallel", "parallel")),
        name="sb_attention",
    )(sb, sb, sb)


def _head_rms(x, g, lane):
    xx = x * x
    lo = lane < HEAD_DIM
    s_lo = jnp.sum(jnp.where(lo, xx, 0.0), axis=-1, keepdims=True)
    s_hi = jnp.sum(jnp.where(lo, 0.0, xx), axis=-1, keepdims=True)
    ms = jnp.where(lo, s_lo, s_hi) * (1.0 / HEAD_DIM)
    return x * lax.rsqrt(ms + EPS) * g


def _dil_kernel(q_ref, k_ref, v_ref, gq_ref, gk_ref, o_ref, l_ref, qn_ref, kn_ref, vn_ref,
                *, dilation, length, slopes, scale, unroll):
    nb = length // DIL_BLOCK
    pair = pl.program_id(1)
    lane_l = lax.broadcasted_iota(jnp.int32, (length, LANES), 1)
    lane_b = lax.broadcasted_iota(jnp.int32, (DIL_BLOCK, LANES), 1)
    qi = lax.broadcasted_iota(jnp.int32, (DIL_BLOCK, 2 * DIL_BLOCK), 0)
    kj = lax.broadcasted_iota(jnp.int32, (DIL_BLOCK, 2 * DIL_BLOCK), 1) - DIL_BLOCK
    gap = qi - kj
    band = (gap >= 0) & (gap <= DIL_BLOCK)
    gapf = (gap * dilation).astype(F32)
    bias_rest, bias_first = [], []
    for half in range(2):
        slope = jnp.where(pair == 0, slopes[half], slopes[2 + half])
        bias_rest.append(jnp.where(band, -(slope * gapf), NEG_BIG))
        bias_first.append(jnp.where(band & (kj >= 0), -(slope * gapf), NEG_BIG))

    zero_block = jnp.zeros((DIL_BLOCK, LANES), BF16)
    for c in range(dilation):
        rows = pl.ds(0, length) if dilation == 1 else pl.ds(c, length, stride=dilation)
        qn_ref[c] = _head_rms(q_ref[rows, :], gq_ref[...], lane_l).astype(BF16) * scale
        kn_ref[c, pl.ds(0, DIL_BLOCK), :] = zero_block
        vn_ref[c, pl.ds(0, DIL_BLOCK), :] = zero_block
        kn_ref[c, pl.ds(DIL_BLOCK, length), :] = _head_rms(k_ref[rows, :], gk_ref[...], lane_l).astype(BF16)
        vn_ref[c, pl.ds(DIL_BLOCK, length), :] = v_ref[rows, :].astype(BF16)

    def block(b):
        c, n = b // nb, b % nb
        start = pl.multiple_of(n * DIL_BLOCK, DIL_BLOCK)
        qb = qn_ref[c, pl.ds(start, DIL_BLOCK), :]
        kw = kn_ref[c, pl.ds(start, 2 * DIL_BLOCK), :]
        vw = vn_ref[c, pl.ds(start, 2 * DIL_BLOCK), :]
        outs, lses = [], []
        for half in range(2):
            in_head = (lane_b < HEAD_DIM) if half == 0 else (lane_b >= HEAD_DIM)
            qm = jnp.where(in_head, qb, jnp.zeros_like(qb))
            s = lax.dot_general(qm, kw, (((1,), (1,)), ((), ())), preferred_element_type=F32)
            s = s + jnp.where(n == 0, bias_first[half], bias_rest[half])
            m = jnp.max(s, axis=-1, keepdims=True)
            e = jnp.exp(s - m)
            denom = jnp.sum(e, axis=-1, keepdims=True)
            o = jnp.dot(e.astype(BF16), vw, preferred_element_type=F32) / denom
            outs.append(o)
            lses.append(m + jnp.log(denom))
        lo = lane_b < HEAD_DIM
        if dilation == 1:
            dst = pl.ds(start, DIL_BLOCK)
        else:
            dst = pl.ds(c + start * dilation, DIL_BLOCK, stride=dilation)
        o_ref[dst, :] = jnp.where(lo, outs[0], outs[1])
        l_ref[dst, :] = jnp.where(lo, lses[0], lses[1])

    def blocks(g, _):
        for k in range(unroll):
            block(g * unroll + k)
        return 0

    lax.fori_loop(0, dilation * nb // unroll, blocks, 0)


def _dil_group(dil, g_q2, g_k2, group, batch, seq):
    _, dilation = DIL_CONFIG[group]
    length = seq // dilation
    blocks = DIL_WIDTH // LANES
    pairs = DIL_OUT_WIDTH // LANES
    all_slopes = [2.0 ** (-ALIBI_MAX_BIAS * (i + 1) / DIL_HEADS) for i in range(DIL_HEADS)]
    slopes = tuple(float(np.float32(s)) for s in
                   all_slopes[group * DIL_HEADS_PER_GROUP:(group + 1) * DIL_HEADS_PER_GROUP])
    kern = functools.partial(_dil_kernel, dilation=dilation, length=length, slopes=slopes,
                             scale=HEAD_DIM ** -0.5, unroll=8)
    col = lambda off: (lambda b, p: (b, 0, off + group * pairs + p))
    out_sd = jax.ShapeDtypeStruct((batch, seq, DIL_OUT_WIDTH), F32)
    return pl.pallas_call(
        kern,
        grid=(batch, pairs),
        in_specs=[pl.BlockSpec((None, seq, LANES), col(0)),
                  pl.BlockSpec((None, seq, LANES), col(blocks)),
                  pl.BlockSpec((None, seq, LANES), col(2 * blocks)),
                  pl.BlockSpec((1, LANES), lambda b, p: (0, 0)),
                  pl.BlockSpec((1, LANES), lambda b, p: (0, 0))],
        out_specs=[pl.BlockSpec((None, seq, LANES), lambda b, p: (b, 0, p)),
                   pl.BlockSpec((None, seq, LANES), lambda b, p: (b, 0, p))],
        out_shape=[out_sd, out_sd],
        scratch_shapes=[pltpu.VMEM((dilation, length, LANES), BF16),
                        pltpu.VMEM((dilation, length + DIL_BLOCK, LANES), BF16),
                        pltpu.VMEM((dilation, length + DIL_BLOCK, LANES), BF16)],
        compiler_params=_cparams(("parallel", "parallel")),
        name=f"dilated_attention_r{dilation}",
    )(dil, dil, dil, g_q2, g_k2)


def _mem_kernel(q_ref, kv_ref, gq_ref, gk_ref, o_ref, *, scale):
    for h in range(MEM_HEADS):
        cols = slice(h * MEM_HEAD_DIM, (h + 1) * MEM_HEAD_DIM)
        vcols = slice(MEM_WIDTH + h * MEM_HEAD_DIM, MEM_WIDTH + (h + 1) * MEM_HEAD_DIM)
        q = _rms(q_ref[:, cols], gq_ref[...]).astype(BF16)
        k = _rms(kv_ref[:, cols], gk_ref[...]).astype(BF16)
        s = lax.dot_general(q, k, (((1,), (1,)), ((), ())), preferred_element_type=F32) * scale
        m = jnp.max(s, axis=-1, keepdims=True)
        e = jnp.exp(s - m)
        denom = jnp.sum(e, axis=-1, keepdims=True)
        o = jnp.dot(e.astype(BF16), kv_ref[:, vcols].astype(BF16), preferred_element_type=F32) / denom
        o_ref[:, cols] = o.astype(o_ref.dtype)


def _mem_attention(mq, kv, g_q, g_k, batch, seq, mem_len, tq=512):
    return pl.pallas_call(
        functools.partial(_mem_kernel, scale=MEM_HEAD_DIM ** -0.5),
        grid=(batch, seq // tq),
        in_specs=[pl.BlockSpec((None, tq, MEM_WIDTH), lambda b, i: (b, i, 0)),
                  pl.BlockSpec((None, mem_len, 2 * MEM_WIDTH), lambda b, i: (b, 0, 0)),
                  pl.BlockSpec((1, LANES), lambda b, i: (0, 0)),
                  pl.BlockSpec((1, LANES), lambda b, i: (0, 0))],
        out_specs=pl.BlockSpec((None, tq, MEM_WIDTH), lambda b, i: (b, i, 0)),
        out_shape=jax.ShapeDtypeStruct((batch, seq, MEM_WIDTH), BF16),
        compiler_params=_cparams(("parallel", "parallel")),
        name="memory_attention",
    )(mq, kv, g_q.reshape(1, LANES), g_k.reshape(1, LANES))


def _merge_kernel(x_ref, g_ref, ysb_ref, o0_ref, o1_ref, o2_ref, l0_ref, l1_ref, l2_ref, ymem_ref,
                  wg_ref, bg_ref, wsb_ref, wdil_ref, wmem_ref, wout_ref, out_ref):
    d = x_ref.shape[-1]
    x = x_ref[...]
    h = _rms(x, g_ref[...]).astype(BF16)
    l0, l1, l2 = l0_ref[...], l1_ref[...], l2_ref[...]
    m = jnp.maximum(jnp.maximum(l0, l1), l2)
    e0, e1, e2 = jnp.exp(l0 - m), jnp.exp(l1 - m), jnp.exp(l2 - m)
    inv = 1.0 / (e0 + e1 + e2)
    y_dil = ((e0 * inv) * o0_ref[...] + (e1 * inv) * o1_ref[...] + (e2 * inv) * o2_ref[...]).astype(BF16)
    merged = jnp.zeros(x.shape, F32)
    branches = ((ysb_ref[...], wsb_ref), (y_dil, wdil_ref), (ymem_ref[...], wmem_ref))
    for n, (y, w_ref) in enumerate(branches):
        pre = jnp.dot(h, wg_ref[:, n * d:(n + 1) * d], preferred_element_type=F32) + bg_ref[:, n * d:(n + 1) * d]
        gate = jax.nn.sigmoid(pre)
        merged = merged + gate * jnp.dot(y, w_ref[...], preferred_element_type=F32)
    out_ref[...] = x + jnp.dot(merged.astype(BF16), wout_ref[...], preferred_element_type=F32)


def _merge(x2d, g_mix, y_sb, o_list, l_list, y_mem, w_gate, b_gate, w_o_sb, w_o_dil, w_o_mem, w_out, tm=256):
    t, d = x2d.shape
    row = lambda w: pl.BlockSpec((tm, w), lambda i: (i, 0))
    full = lambda a: pl.BlockSpec(a.shape, lambda i: (0, 0))
    g2 = g_mix.reshape(1, d)
    b2 = b_gate.reshape(1, -1)
    args = [x2d, g2, y_sb, *o_list, *l_list, y_mem, w_gate, b2, w_o_sb, w_o_dil, w_o_mem, w_out]
    specs = [row(d), full(g2), row(SB_WIDTH)] + [row(DIL_OUT_WIDTH)] * 6 + [row(MEM_WIDTH)] + \
            [full(a) for a in (w_gate, b2, w_o_sb, w_o_dil, w_o_mem, w_out)]
    return pl.pallas_call(
        _merge_kernel,
        grid=(t // tm,),
        in_specs=specs,
        out_specs=row(d),
        out_shape=jax.ShapeDtypeStruct((t, d), F32),
        compiler_params=_cparams(("parallel",)),
        name="merge_project",
    )(*args)


def _oddeven_merge_sort_pairs(n):
    pairs = []

    def merge(lo, hi, r):
        step = r * 2
        if step < hi - lo:
            merge(lo, hi, step)
            merge(lo + r, hi, step)
            for i in range(lo + r, hi - r, step):
                pairs.append((i, i + r))
        else:
            pairs.append((lo, lo + r))

    def sort(lo, hi):
        if hi - lo >= 1:
            mid = lo + (hi - lo) // 2
            sort(lo, mid)
            sort(mid + 1, hi)
            merge(lo, hi, 1)

    sort(0, n - 1)
    return pairs


_SORT16 = _oddeven_merge_sort_pairs(PEER_TOPK)
_CAND = [(a, b) for a in range(PEER_TOPK) for b in range(PEER_TOPK) if (a + 1) * (b + 1) <= PEER_TOPK]


def _top16_sorted(sc):
    n = PEER_KEYS // 8
    v = [sc[8 * i:8 * (i + 1), :] for i in range(n)]
    for (i, j) in _SORT16:
        hi, lo = jnp.maximum(v[i], v[j]), jnp.minimum(v[i], v[j])
        v[i], v[j] = hi, lo
    for shift in (4, 2, 1):
        other = [pltpu.roll(x, shift, 0) for x in v]
        v = [jnp.maximum(v[i], other[n - 1 - i]) for i in range(n)]
        d = n // 2
        while d >= 1:
            for i in range(n):
                if (i // d) % 2 == 0:
                    hi, lo = jnp.maximum(v[i], v[i + d]), jnp.minimum(v[i], v[i + d])
                    v[i], v[i + d] = hi, lo
            d //= 2
    return v


def _exchange(v, i, j):
    a, b = v[i], v[j]
    if b is None:
        return
    if a is None:
        v[i], v[j] = b, None
    else:
        v[i], v[j] = jnp.maximum(a, b), jnp.minimum(a, b)


def _bitonic_descending(v):
    n = len(v)
    d = n // 2
    while d >= 1:
        for i in range(n):
            if (i // d) % 2 == 0:
                _exchange(v, i, i + d)
        d //= 2


def _largest16(vals):
    n = PEER_TOPK
    groups = []
    for g0 in range(0, len(vals), n):
        g = list(vals[g0:g0 + n]) + [None] * max(0, g0 + n - len(vals))
        for (i, j) in _SORT16:
            _exchange(g, i, j)
        groups.append(g)
    while len(groups) > 1:
        merged = []
        for k in range(0, len(groups) - 1, 2):
            a, b = groups[k], groups[k + 1]
            m = []
            for i in range(n):
                x, y = a[i], b[n - 1 - i]
                m.append(x if y is None else y if x is None else jnp.maximum(x, y))
            _bitonic_descending(m)
            merged.append(m)
        if len(groups) % 2:
            merged.append(groups[-1])
        groups = merged
    return groups[0]


def _peer_select_kernel(x_ref, g_ref, wq_ref, sk_ref, h2t_ref, c1_ref, e1_ref, r2_ref, e2_ref,
                        q_ref, sc_ref, top_ref, cand_ref, tau_ref, invz_ref):
    tm = x_ref.shape[0]
    h2f = _rms(x_ref[...], g_ref[...])
    h2 = h2f.astype(BF16)
    h2t_ref[...] = h2f.T.astype(BF16)
    q_ref[...] = jnp.dot(h2, wq_ref[...], preferred_element_type=F32).astype(BF16)
    for hp in range(2 * PEER_HEADS):
        sc_ref[hp] = lax.dot_general(sk_ref[hp], q_ref[:, hp * PEER_HALF:(hp + 1) * PEER_HALF],
                                     (((1,), (1,)), ((), ())), preferred_element_type=F32)

    def sort_head(h, _):
        for p in range(2):
            top = _top16_sorted(sc_ref[2 * h + p])
            for a in range(PEER_TOPK):
                top_ref[p, a, pl.ds(h, 1), :] = top[a][0:1, :]
        return 0

    lax.fori_loop(0, PEER_HEADS, sort_head, 0)

    for n, (a, b) in enumerate(_CAND):
        cand_ref[n] = top_ref[0, a] + top_ref[1, b]

    lanes = 2 * LANES
    for c0 in range(0, tm, lanes):
        top = _largest16([cand_ref[n, :, c0:c0 + lanes] for n in range(len(_CAND))])
        tau_ref[:, c0:c0 + lanes] = top[PEER_TOPK - 1]
    tau = tau_ref[...]
    top_sum = cand_ref[0]
    z = jnp.zeros((8, tm), F32)
    for n in range(len(_CAND)):
        ci = cand_ref[n]
        z = z + jnp.where(ci >= tau, jnp.exp(ci - top_sum), 0.0)
    invz_ref[...] = 0.5 / z

    def emit(h, _):
        row = pl.ds(h, 1)
        s1, s2 = sc_ref[2 * h], sc_ref[2 * h + 1]
        tau_h = tau_ref[row, :]
        cnt = jnp.zeros(s1.shape, F32)
        rank = jnp.zeros(s2.shape, F32)
        for b in range(PEER_TOPK):
            s2b = top_ref[1, b, row, :]
            cnt = jnp.where((s1 + s2b) >= tau_h, float(b + 1), cnt)
            rank = jnp.where(s2b > s2, float(b + 1), rank)
        in_top = s1 >= top_ref[0, PEER_TOPK - 1, row, :]
        c1_ref[h] = jnp.where(in_top, cnt, 0.0)
        e1_ref[h] = jnp.exp(s1 - top_ref[0, 0, row, :]) * invz_ref[row, :]
        r2_ref[h] = rank
        e2_ref[h] = jnp.exp(s2 - top_ref[1, 0, row, :])
        return 0

    lax.fori_loop(0, PEER_HEADS, emit, 0)


def _peer_select(x1, g_ffn, w_q, subkeys, tm=512):
    t, d = x1.shape
    halves = 2 * PEER_HEADS
    sk3 = subkeys.reshape(halves, PEER_KEYS, PEER_HALF)
    shape = (PEER_HEADS, PEER_KEYS, t)
    big_spec = pl.BlockSpec((PEER_HEADS, PEER_KEYS, tm), lambda i: (0, 0, i))
    return pl.pallas_call(
        _peer_select_kernel,
        grid=(t // tm,),
        in_specs=[pl.BlockSpec((tm, d), lambda i: (i, 0)),
                  pl.BlockSpec((1, d), lambda i: (0, 0)),
                  pl.BlockSpec(w_q.shape, lambda i: (0, 0)),
                  pl.BlockSpec(sk3.shape, lambda i: (0, 0, 0))],
        out_specs=[pl.BlockSpec((d, tm), lambda i: (0, i)), big_spec, big_spec, big_spec, big_spec],
        out_shape=[jax.ShapeDtypeStruct((d, t), BF16),
                   jax.ShapeDtypeStruct(shape, F32), jax.ShapeDtypeStruct(shape, F32),
                   jax.ShapeDtypeStruct(shape, F32), jax.ShapeDtypeStruct(shape, F32)],
        scratch_shapes=[pltpu.VMEM((tm, halves * PEER_HALF), BF16),
                        pltpu.VMEM((halves, PEER_KEYS, tm), F32),
                        pltpu.VMEM((2, PEER_TOPK, 8, tm), F32),
                        pltpu.VMEM((len(_CAND), 8, tm), F32),
                        pltpu.VMEM((8, tm), F32),
                        pltpu.VMEM((8, tm), F32)],
        compiler_params=_cparams(("parallel",)),
        name="peer_select",
    )(x1, g_ffn.reshape(1, d), w_q, sk3)


SUBLANES = 8
GATE_SHARE = 1
DENSE_ROWS = 8


def _peer_dense_kernel(x_ref, h2t_ref, u_ref, vt_ref, c1_ref, e1_ref, r2_ref, e2_ref,
                       out_ref, acc_ref, w_ref, act_ref, *, rows, chunk):
    j = pl.program_id(1)
    tm = h2t_ref.shape[1]

    @pl.when(j == 0)
    def _():
        acc_ref[...] = jnp.zeros(acc_ref.shape, F32)

    act_ref[...] = jnp.dot(u_ref[...], h2t_ref[...], preferred_element_type=F32)
    sqrt_half = math.sqrt(0.5)
    n_tiles = PEER_KEYS // SUBLANES
    for c in range(tm // chunk):
        tok = slice(c * chunk, (c + 1) * chunk)
        for r0 in range(0, rows, GATE_SHARE):
            gate = [[None] * n_tiles for _ in range(GATE_SHARE)]
            for h in range(PEER_HEADS):
                c1g = c1_ref[h, :, tok]
                e1g = e1_ref[h, :, tok]
                c1 = [jnp.broadcast_to(c1g[r0 + k:r0 + k + 1, :], (SUBLANES, chunk)) for k in range(GATE_SHARE)]
                e1 = [jnp.broadcast_to(e1g[r0 + k:r0 + k + 1, :], (SUBLANES, chunk)) for k in range(GATE_SHARE)]
                for m in range(n_tiles):
                    r2 = r2_ref[h, m * SUBLANES:(m + 1) * SUBLANES, tok]
                    e2 = e2_ref[h, m * SUBLANES:(m + 1) * SUBLANES, tok]
                    for k in range(GATE_SHARE):
                        term = jnp.where(r2 < c1[k], e2 * e1[k], 0.0)
                        gate[k][m] = term if h == 0 else gate[k][m] + term
            for k in range(GATE_SHARE):
                base = (r0 + k) * PEER_KEYS
                for m in range(0, n_tiles, 2):
                    rows_ = slice(base + m * SUBLANES, base + (m + 2) * SUBLANES)
                    a = act_ref[rows_, tok]
                    gelu2 = a * (1.0 + lax.erf(a * sqrt_half))
                    g = jnp.concatenate([gate[k][m], gate[k][m + 1]], axis=0)
                    w_ref[rows_, tok] = (g * gelu2).astype(BF16)
    acc_ref[...] += jnp.dot(vt_ref[...], w_ref[...], preferred_element_type=F32)

    @pl.when(j == pl.num_programs(1) - 1)
    def _():
        out_ref[...] = x_ref[...] + acc_ref[...].T


def _peer_dense(x1, h2t, u, vt, c1, e1, r2, e2, tm=1024, rows=DENSE_ROWS, chunk=128):
    t, d = x1.shape
    n_exp = u.shape[0]
    tn = rows * PEER_KEYS
    big_spec = pl.BlockSpec((PEER_HEADS, PEER_KEYS, tm), lambda i, j: (0, 0, i))
    row_spec = pl.BlockSpec((PEER_HEADS, rows, tm), lambda i, j: (0, j, i))
    return pl.pallas_call(
        functools.partial(_peer_dense_kernel, rows=rows, chunk=chunk),
        grid=(t // tm, n_exp // tn),
        in_specs=[pl.BlockSpec((tm, d), lambda i, j: (i, 0)),
                  pl.BlockSpec((d, tm), lambda i, j: (0, i)),
                  pl.BlockSpec((tn, d), lambda i, j: (j, 0)),
                  pl.BlockSpec((None, d, tn), lambda i, j: (j, 0, 0)),
                  row_spec, row_spec, big_spec, big_spec],
        out_specs=pl.BlockSpec((tm, d), lambda i, j: (i, 0)),
        out_shape=jax.ShapeDtypeStruct((t, d), F32),
        scratch_shapes=[pltpu.VMEM((d, tm), F32), pltpu.VMEM((tn, tm), BF16), pltpu.VMEM((tn, tm), F32)],
        compiler_params=_cparams(("parallel", "arbitrary")),
        name="peer_dense",
    )(x1, h2t, u, vt, c1, e1, r2, e2)


def kernel(x, mem, g_mix, g_mem, w_in, w_mem_kv, g_q_dil, g_k_dil, g_q_mem, g_k_mem, w_o_sb, w_o_dil,
           w_o_mem, w_gate, b_gate, w_out, g_ffn, w_peer_q, peer_subkeys, peer_u, peer_v):
    batch, seq, d = x.shape
    mem_len = mem.shape[1]
    depth = w_in.shape[0]
    xt = x.reshape(batch * seq, d)
    for l in range(depth):
        w_in_l = w_in[l].astype(BF16)
        c_sb, c_dil = 3 * SB_WIDTH, 3 * SB_WIDTH + 3 * DIL_WIDTH
        sb, dil, mq = _in_proj(xt, g_mix[l], w_in_l, (c_sb, c_dil - c_sb, MEM_WIDTH), (BF16, F32, F32))
        kv = _norm_matmul(mem.reshape(batch * mem_len, d), g_mem[l], w_mem_kv[l].astype(BF16), F32, 512, 512)

        y_sb = _sb_attention(sb.reshape(batch, seq, c_sb), batch, seq)

        g_q2 = jnp.tile(g_q_dil[l], 2).reshape(1, LANES)
        g_k2 = jnp.tile(g_k_dil[l], 2).reshape(1, LANES)
        dil3 = dil.reshape(batch, seq, 3 * DIL_WIDTH)
        o_list, l_list = [], []
        for group in range(len(DIL_CONFIG)):
            o, lse = _dil_group(dil3, g_q2, g_k2, group, batch, seq)
            o_list.append(o.reshape(batch * seq, DIL_OUT_WIDTH))
            l_list.append(lse.reshape(batch * seq, DIL_OUT_WIDTH))

        y_mem = _mem_attention(mq.reshape(batch, seq, MEM_WIDTH), kv.reshape(batch, mem_len, 2 * MEM_WIDTH),
                               g_q_mem[l], g_k_mem[l], batch, seq, mem_len)

        x1 = _merge(xt, g_mix[l], y_sb.reshape(batch * seq, SB_WIDTH), o_list, l_list,
                    y_mem.reshape(batch * seq, MEM_WIDTH), w_gate[l].astype(BF16), b_gate[l],
                    w_o_sb[l].astype(BF16), w_o_dil[l].astype(BF16), w_o_mem[l].astype(BF16),
                    w_out[l].astype(BF16))

        h2t, c1, e1, r2, e2 = _peer_select(x1, g_ffn[l], w_peer_q[l].astype(BF16),
                                          peer_subkeys[l].astype(BF16))
        n_exp = peer_v.shape[1]
        tn = DENSE_ROWS * PEER_KEYS
        vt = peer_v[l].astype(BF16).reshape(n_exp // tn, tn, d).transpose(0, 2, 1)
        xt = _peer_dense(x1, h2t, peer_u[l].astype(BF16), vt, c1, e1, r2, e2)
    return xt.reshape(batch, seq, d)
```
